```python
import math
import jax
import jax.numpy as jnp
from jax import lax
import numpy as np

D_MODEL = 1024
BATCH = 16
SEQ = 4096
DEPTH = 1
DEC_BATCH = 128
DEC_SEQ = 8
PAST_LEN = 8192
PAGE_SIZE = 128

HEAD_DIM = 64
MIX_WIDTH = D_MODEL
DIFF_WIDTH = MIX_WIDTH // 2
FOX_WIDTH = MIX_WIDTH - DIFF_WIDTH
DIFF_HEADS = DIFF_WIDTH // (2 * HEAD_DIM)
FOX_HEADS = FOX_WIDTH // HEAD_DIM
N_IN = 3 * DIFF_WIDTH + 3 * FOX_WIDTH + FOX_HEADS
ROPE_THETA = 10000.0
Q_BLOCK = 128
N_EXPERTS = 32
TOP_K = 4
D_FF = D_MODEL
SWIGLU_ALPHA = 1.702
SWIGLU_LIMIT = 7.0
EXPERT_BLOCK = 128
NORM_EPS = 1e-5
N_MOD = 6
FORGET_BIAS_CENTER = 4.0

kernel_name = 'hymba_diff_fox_moe_step'


def rmsnorm(x, g):
    xf = x.astype(jnp.float32)
    y = xf * lax.rsqrt(jnp.mean(xf * xf, axis=-1, keepdims=True) + NORM_EPS)
    return (y * g.astype(jnp.float32)).astype(x.dtype)


def adaln(c, w, b):
    m = jax.nn.silu(c) @ w + b
    return jnp.split(m[:, None, :], N_MOD, axis=-1)


def rope(x, pos):
    half = HEAD_DIM // 2
    inv = 1.0 / (ROPE_THETA ** (jnp.arange(half, dtype=jnp.float32) / half))
    ang = pos.astype(jnp.float32)[:, None] * inv[None, :]
    cos = jnp.cos(ang)[:, None, None, :]
    sin = jnp.sin(ang)[:, None, None, :]
    xf = x.astype(jnp.float32)
    x1, x2 = xf[..., :half], xf[..., half:]
    return jnp.concatenate([x1 * cos - x2 * sin, x2 * cos + x1 * sin], axis=-1).astype(x.dtype)


def project_mixers(h, w_in_l, b_f_l, pos):
    b, s, _ = h.shape
    z = h @ w_in_l
    cuts = np.cumsum([DIFF_WIDTH] * 3 + [FOX_WIDTH] * 3).tolist()
    qd, kd, vd, qf, kf, vf, fl = jnp.split(z, cuts, axis=-1)
    qd = rope(qd.reshape(b, s, DIFF_HEADS, 2, HEAD_DIM), pos)
    kd = rope(kd.reshape(b, s, DIFF_HEADS, 2, HEAD_DIM), pos)
    vd = vd.reshape(b, s, DIFF_HEADS, 2 * HEAD_DIM)
    qf = qf.reshape(b, s, FOX_HEADS, HEAD_DIM)
    kf = kf.reshape(b, s, FOX_HEADS, HEAD_DIM)
    vf = vf.reshape(b, s, FOX_HEADS, HEAD_DIM)
    logf = jax.nn.log_sigmoid((fl + b_f_l).astype(jnp.float32))
    return qd, kd, vd, qf, kf, vf, logf


def diff_attention_prompt(qd, kd, vd, lam):
    b, s = qd.shape[:2]
    n_blk = s // Q_BLOCK
    q_blocks = jnp.moveaxis(qd.reshape(b, n_blk, Q_BLOCK, DIFF_HEADS, 2, HEAD_DIM), 1, 0)
    k_pos = jnp.arange(s)
    scale = HEAD_DIM ** -0.5

    def one_block(args):
        q_blk, i = args
        q_pos = i * Q_BLOCK + jnp.arange(Q_BLOCK)
        logits = jnp.einsum('bqhcd,bkhcd->bhcqk', q_blk, kd).astype(jnp.float32) * scale
        logits = jnp.where(k_pos[None, :] <= q_pos[:, None], logits, -jnp.inf)
        p = jax.nn.softmax(logits, axis=-1)
        a = p[:, :, 0] - lam * p[:, :, 1]
        return jnp.einsum('bhqk,bkhe->bqhe', a.astype(vd.dtype), vd)

    out = lax.map(one_block, (q_blocks, jnp.arange(n_blk)))
    return jnp.moveaxis(out, 0, 1).reshape(b, s, DIFF_HEADS, 2 * HEAD_DIM)


def fox_attention_prompt(qf, kf, vf, logf):
    b, s = qf.shape[:2]
    n_blk = s // Q_BLOCK
    cum_k = jnp.transpose(jnp.cumsum(logf, axis=1), (0, 2, 1))
    q_blocks = jnp.moveaxis(qf.reshape(b, n_blk, Q_BLOCK, FOX_HEADS, HEAD_DIM), 1, 0)
    c_blocks = jnp.moveaxis(cum_k.reshape(b, FOX_HEADS, n_blk, Q_BLOCK), 2, 0)
    k_pos = jnp.arange(s)
    scale = HEAD_DIM ** -0.5

    def one_block(args):
        q_blk, c_blk, i = args
        q_pos = i * Q_BLOCK + jnp.arange(Q_BLOCK)
        logits = (jnp.einsum('bqhd,bkhd->bhqk', q_blk, kf).astype(jnp.float32) * scale
                  + c_blk[..., None] - cum_k[:, :, None, :])
        logits = jnp.where(k_pos[None, :] <= q_pos[:, None], logits, -jnp.inf)
        p = jax.nn.softmax(logits, axis=-1)
        return jnp.einsum('bhqk,bkhd->bqhd', p.astype(vf.dtype), vf)

    out = lax.map(one_block, (q_blocks, c_blocks, jnp.arange(n_blk)))
    return jnp.moveaxis(out, 0, 1).reshape(b, s, FOX_HEADS, HEAD_DIM)


def attention_sample(qd, kd, vd, qf, kf, vf, logf, page_table, ck_d, cv_d, ck_f, cv_f, c_lf, layer, lam):
    past = page_table.shape[1] * ck_d.shape[2]
    t = qd.shape[1]
    k_pos = jnp.arange(past + t)
    q_pos = past + jnp.arange(t)
    mask = k_pos[None, :] <= q_pos[:, None]
    scale = HEAD_DIM ** -0.5

    def one_seq(args):
        pt, qd_s, kd_s, vd_s, qf_s, kf_s, vf_s, lf_s = args
        kd_all = jnp.concatenate([ck_d[layer, pt].reshape(past, DIFF_HEADS, 2, HEAD_DIM).astype(kd_s.dtype), kd_s], axis=0)
        vd_all = jnp.concatenate([cv_d[layer, pt].reshape(past, DIFF_HEADS, 2 * HEAD_DIM).astype(vd_s.dtype), vd_s], axis=0)
        logits = jnp.einsum('qhcd,khcd->hcqk', qd_s, kd_all).astype(jnp.float32) * scale
        p = jax.nn.softmax(jnp.where(mask, logits, -jnp.inf), axis=-1)
        a = p[:, 0] - lam * p[:, 1]
        od = jnp.einsum('hqk,khe->qhe', a.astype(vd_all.dtype), vd_all)
        kf_all = jnp.concatenate([ck_f[layer, pt].reshape(past, FOX_HEADS, HEAD_DIM).astype(kf_s.dtype), kf_s], axis=0)
        vf_all = jnp.concatenate([cv_f[layer, pt].reshape(past, FOX_HEADS, HEAD_DIM).astype(vf_s.dtype), vf_s], axis=0)
        lf_all = jnp.concatenate([c_lf[layer, pt].reshape(past, FOX_HEADS).astype(jnp.float32), lf_s.astype(jnp.float32)], axis=0)
        cum = jnp.cumsum(lf_all, axis=0)
        logits = (jnp.einsum('qhd,khd->hqk', qf_s, kf_all).astype(jnp.float32) * scale
                  + cum[past:].T[:, :, None] - cum.T[:, None, :])
        p = jax.nn.softmax(jnp.where(mask, logits, -jnp.inf), axis=-1)
        of = jnp.einsum('hqk,khd->qhd', p.astype(vf_all.dtype), vf_all)
        return od, of

    return lax.map(one_seq, (page_table, qd, kd, vd, qf, kf, vf, logf))


def mixer_output(od, of, g_sub, lam_init, w_out_l):
    b, s = od.shape[:2]
    od = rmsnorm(od, g_sub) * (1.0 - lam_init)
    o = jnp.concatenate([od.reshape(b, s, DIFF_WIDTH), of.reshape(b, s, FOX_WIDTH)], axis=-1)
    return o @ w_out_l


def moe_ffn(h, w_r, b_r, w_gu_l, b_gu_l, w_down_l, b_down_l):
    n_tok, d = h.shape
    logits = (h @ w_r + b_r).astype(jnp.float32)
    top_val, top_idx = lax.top_k(logits, TOP_K)
    gates = jax.nn.softmax(top_val, axis=-1)
    n_asg = n_tok * TOP_K
    flat_e = top_idx.reshape(-1).astype(jnp.int32)
    order = jnp.argsort(flat_e)
    e_sorted = flat_e[order]
    tok_sorted = (order // TOP_K).astype(jnp.int32)
    gate_sorted = gates.reshape(-1)[order]
    counts = jnp.bincount(flat_e, length=N_EXPERTS)
    padded = (counts + EXPERT_BLOCK - 1) // EXPERT_BLOCK * EXPERT_BLOCK
    pad_end = jnp.cumsum(padded)
    pad_start = pad_end - padded
    start = jnp.cumsum(counts) - counts
    slot = pad_start[e_sorted] + jnp.arange(n_asg) - start[e_sorted]
    n_blocks = -(-n_asg // EXPERT_BLOCK) + N_EXPERTS
    n_slots = n_blocks * EXPERT_BLOCK
    slot_tok = jnp.full((n_slots,), n_tok, jnp.int32).at[slot].set(tok_sorted)
    slot_gate = jnp.zeros((n_slots,), jnp.float32).at[slot].set(gate_sorted)
    block_expert = jnp.minimum(jnp.searchsorted(pad_end, jnp.arange(n_blocks) * EXPERT_BLOCK, side='right'), N_EXPERTS - 1)
    h_pad = jnp.concatenate([h, jnp.zeros((1, d), h.dtype)], axis=0)

    def expert_block(args):
        toks, e = args
        gu = h_pad[toks] @ w_gu_l[e] + b_gu_l[e]
        g, u = jnp.split(gu, 2, axis=-1)
        g = jnp.minimum(g, SWIGLU_LIMIT)
        u = jnp.clip(u, -SWIGLU_LIMIT, SWIGLU_LIMIT)
        act = (u + 1.0) * g * jax.nn.sigmoid(SWIGLU_ALPHA * g)
        return act @ w_down_l[e] + b_down_l[e]

    y = lax.map(expert_block, (slot_tok.reshape(n_blocks, EXPERT_BLOCK), block_expert))
    y = y.reshape(n_slots, d) * slot_gate[:, None].astype(h.dtype)
    return jnp.zeros((n_tok + 1, d), h.dtype).at[slot_tok].add(y)[:n_tok]


def trunk_layer(x, c, pos, attend, w_ada_l, b_ada_l, g1, g2, w_in_l, b_f_l, g_sub, lam_init, w_out_l,
                w_r, b_r, w_gu_l, b_gu_l, w_down_l, b_down_l):
    sh1, sc1, ga1, sh2, sc2, ga2 = adaln(c, w_ada_l, b_ada_l)
    h = rmsnorm(x, g1) * (1.0 + sc1) + sh1
    qd, kd, vd, qf, kf, vf, lf = project_mixers(h, w_in_l, b_f_l, pos)
    od, of = attend(qd, kd, vd, qf, kf, vf, lf)
    x = x + ga1 * mixer_output(od, of, g_sub, lam_init, w_out_l)
    h = rmsnorm(x, g2) * (1.0 + sc2) + sh2
    b, s, d = x.shape
    x = x + ga2 * moe_ffn(h.reshape(b * s, d), w_r, b_r, w_gu_l, b_gu_l, w_down_l, b_down_l).reshape(b, s, d)
    rows = (kd.reshape(b, s, DIFF_HEADS, 2 * HEAD_DIM), vd, kf, vf, lf)
    return x, rows


def setup_inputs(seed: int = 0) -> dict:
    key = jax.random.key(seed)
    k = jax.random.split(key, 32)
    f32 = jnp.float32
    d = D_MODEL
    n_pages = PAST_LEN // PAGE_SIZE
    n_used = DEC_BATCH * n_pages
    n_pool = n_used + n_used // 4
    pool = (DEPTH, n_pool, PAGE_SIZE)

    def nrm(kk, shape, s=1.0):
        return jax.random.normal(kk, shape, f32) * s

    page_table = jax.random.permutation(k[9], n_pool)[:n_used].reshape(DEC_BATCH, n_pages).astype(jnp.int32)
    return {
        'x_prompt': nrm(k[0], (BATCH, SEQ, d)),
        'x_sample': nrm(k[1], (DEC_BATCH, DEC_SEQ, d)),
        'c_prompt': nrm(k[2], (BATCH, d)),
        'c_sample': nrm(k[3], (DEC_BATCH, d)),
        'cache_k_diff': nrm(k[4], pool + (DIFF_HEADS, 2 * HEAD_DIM)),
        'cache_v_diff': nrm(k[5], pool + (DIFF_HEADS, 2 * HEAD_DIM)),
        'cache_k_fox': nrm(k[6], pool + (FOX_HEADS, HEAD_DIM)),
        'cache_v_fox': nrm(k[7], pool + (FOX_HEADS, HEAD_DIM)),
        'cache_logf_fox': jax.nn.log_sigmoid(FORGET_BIAS_CENTER + nrm(k[8], pool + (FOX_HEADS,))),
        'page_table': page_table,
        'w_ada': nrm(k[10], (DEPTH, d, N_MOD * d), 0.5 * d ** -0.5),
        'b_ada': nrm(k[11], (DEPTH, N_MOD * d), 0.01),
        'g_norm1': 1.0 + nrm(k[12], (DEPTH, d), 0.02),
        'g_norm2': 1.0 + nrm(k[13], (DEPTH, d), 0.02),
        'w_in': nrm(k[14], (DEPTH, d, N_IN), d ** -0.5),
        'b_fgate': jax.random.uniform(k[15], (DEPTH, FOX_HEADS), f32, 2.0, 6.0),
        'lam_q1': nrm(k[16], (DEPTH, HEAD_DIM), 0.1),
        'lam_k1': nrm(k[17], (DEPTH, HEAD_DIM), 0.1),
        'lam_q2': nrm(k[18], (DEPTH, HEAD_DIM), 0.1),
        'lam_k2': nrm(k[19], (DEPTH, HEAD_DIM), 0.1),
        'g_subln': 1.0 + nrm(k[20], (DEPTH, 2 * HEAD_DIM), 0.02),
        'w_out': nrm(k[21], (DEPTH, MIX_WIDTH, d), MIX_WIDTH ** -0.5),
        'w_router': nrm(k[22], (DEPTH, d, N_EXPERTS), d ** -0.5),
        'b_router': nrm(k[23], (DEPTH, N_EXPERTS), 0.01),
        'w_gu': nrm(k[24], (DEPTH, N_EXPERTS, d, 2 * D_FF), d ** -0.5),
        'b_gu': nrm(k[25], (DEPTH, N_EXPERTS, 2 * D_FF), 0.01),
        'w_down': nrm(k[26], (DEPTH, N_EXPERTS, D_FF, d), D_FF ** -0.5),
        'b_down': nrm(k[27], (DEPTH, N_EXPERTS, d), 0.01),
        'g_final': 1.0 + nrm(k[28], (d,), 0.02),
    }


def reference(x_prompt, x_sample, c_prompt, c_sample, cache_k_diff, cache_v_diff, cache_k_fox, cache_v_fox,
              cache_logf_fox, page_table, w_ada, b_ada, g_norm1, g_norm2, w_in, b_fgate, lam_q1, lam_k1,
              lam_q2, lam_k2, g_subln, w_out, w_router, b_router, w_gu, b_gu, w_down, b_down, g_final):
    seq = x_prompt.shape[1]
    dec_seq = x_sample.shape[1]
    past = page_table.shape[1] * cache_k_diff.shape[2]
    pos_prompt = jnp.arange(seq)
    pos_sample = past + jnp.arange(dec_seq)
    xp, xs = x_prompt, x_sample
    rows_p = ([], [], [], [], [])
    rows_s = ([], [], [], [], [])
    for l in range(DEPTH):
        lam_init = 0.8 - 0.6 * math.exp(-0.3 * l)
        lam = (jnp.exp(jnp.sum(lam_q1[l].astype(jnp.float32) * lam_k1[l].astype(jnp.float32)))
               - jnp.exp(jnp.sum(lam_q2[l].astype(jnp.float32) * lam_k2[l].astype(jnp.float32))) + lam_init)
        layer_w = (w_ada[l], b_ada[l], g_norm1[l], g_norm2[l], w_in[l], b_fgate[l], g_subln[l], lam_init,
                   w_out[l], w_router[l], b_router[l], w_gu[l], b_gu[l], w_down[l], b_down[l])

        def attend_prompt(qd, kd, vd, qf, kf, vf, lf, lam=lam):
            return diff_attention_prompt(qd, kd, vd, lam), fox_attention_prompt(qf, kf, vf, lf)

        def attend_sample(qd, kd, vd, qf, kf, vf, lf, lam=lam, l=l):
            return attention_sample(qd, kd, vd, qf, kf, vf, lf, page_table, cache_k_diff, cache_v_diff,
                                    cache_k_fox, cache_v_fox, cache_logf_fox, l, lam)

        xp, new_p = trunk_layer(xp, c_prompt, pos_prompt, attend_prompt, *layer_w)
        xs, new_s = trunk_layer(xs, c_sample, pos_sample, attend_sample, *layer_w)
        for acc, r in zip(rows_p, new_p):
            acc.append(r)
        for acc, r in zip(rows_s, new_s):
            acc.append(r)
    y_prompt = rmsnorm(xp, g_final)
    y_sample = rmsnorm(xs, g_final)
    k_diff_p, v_diff_p, k_fox_p, v_fox_p, logf_p = [jnp.stack(r, axis=0) for r in rows_p]
    k_diff_s, v_diff_s, k_fox_s, v_fox_s, logf_s = [jnp.stack(r, axis=0) for r in rows_s]
    return (y_prompt, y_sample, k_diff_p, v_diff_p, k_fox_p, v_fox_p, logf_p,
            k_diff_s, v_diff_s, k_fox_s, v_fox_s, logf_s)
```

```python
import functools
import math

import jax
import jax.numpy as jnp
import numpy as np
from jax import lax
from jax.experimental import pallas as pl
from jax.experimental.pallas import tpu as pltpu

HEAD_DIM = 64
LANES = 128
ROPE_THETA = 10000.0
NORM_EPS = 1e-5
TOP_K = 4
SWIGLU_ALPHA = 1.702
SWIGLU_LIMIT = 7.0
NEG_BIG = -1e30
ROW_TILE = 512
ATT_TILE = 512
EXPERT_ROWS = 256
PAGES_PER_STEP = 4
VMEM_LIMIT = 56 * 1024 * 1024

F32 = jnp.float32
BF16 = jnp.bfloat16


def _cparams(sem):
    return pltpu.CompilerParams(dimension_semantics=sem, vmem_limit_bytes=VMEM_LIMIT)


def _adaln_body(c_ref, w_ref, b_ref, o_ref):
    c = c_ref[...]
    s = (c * jax.nn.sigmoid(c)).astype(BF16)
    o_ref[...] = jnp.dot(s, w_ref[...].astype(BF16), preferred_element_type=F32) + b_ref[...]


def _adaln(c, w, b):
    n, d = c.shape
    nout = w.shape[1]
    tn = 1024
    return pl.pallas_call(
        _adaln_body,
        grid=(nout // tn,),
        in_specs=[pl.BlockSpec((n, d), lambda j: (0, 0)),
                  pl.BlockSpec((d, tn), lambda j: (0, j)),
                  pl.BlockSpec((1, tn), lambda j: (0, j))],
        out_specs=pl.BlockSpec((n, tn), lambda j: (0, j)),
        out_shape=jax.ShapeDtypeStruct((n, nout), F32),
        compiler_params=_cparams(("arbitrary",)),
        name="adaln",
    )(c, w, b.reshape(1, nout))


def _rms_mod(x, g, sc, sh):
    ms = jnp.mean(x * x, axis=-1, keepdims=True)
    return (x * lax.rsqrt(ms + NORM_EPS) * g) * (1.0 + sc) + sh


def _inproj_body(x_ref, sc_ref, sh_ref, g_ref, cos_ref, sin_ref, w_ref, wf_ref, bf_ref,
                 qd_ref, kd32_ref, kd16_ref, vd32_ref, vd16_ref,
                 qf_ref, kf32_ref, kf16_ref, vf32_ref, vf16_ref, lf_ref):
    tm = x_ref.shape[0]
    hb = _rms_mod(x_ref[...], g_ref[...], sc_ref[...], sh_ref[...]).astype(BF16)
    cos = cos_ref[...]
    sin = sin_ref[...]
    lane = lax.broadcasted_iota(jnp.int32, (tm, LANES), 1)
    first_half = (lane % HEAD_DIM) < (HEAD_DIM // 2)

    def rope(blk):
        partner = jnp.where(first_half, pltpu.roll(blk, LANES - HEAD_DIM // 2, 1),
                            pltpu.roll(blk, HEAD_DIM // 2, 1))
        return blk * cos + partner * sin

    width = qd_ref.shape[1]
    nblk = width // LANES
    outs = ((qd_ref,), (kd32_ref, kd16_ref), (vd32_ref, vd16_ref),
            (qf_ref,), (kf32_ref, kf16_ref), (vf32_ref, vf16_ref))
    for gidx, refs in enumerate(outs):
        z = jnp.dot(hb, w_ref[:, gidx * width:(gidx + 1) * width], preferred_element_type=F32)
        for j in range(nblk):
            blk = z[:, j * LANES:(j + 1) * LANES]
            if gidx < 2:
                blk = rope(blk)
            for r in refs:
                r[:, j * LANES:(j + 1) * LANES] = blk.astype(r.dtype)
    fl = jnp.dot(hb, wf_ref[...], preferred_element_type=F32) + bf_ref[...]
    lf_ref[...] = jax.nn.log_sigmoid(fl)


def _inproj(x, sc, sh, g1, cos, sin, w_main, w_f, b_f, rows_per_group, pos_tiles):
    n, d = x.shape
    tm = ROW_TILE
    width = w_main.shape[1] // 6
    tiles_per_group = rows_per_group // tm if rows_per_group >= tm else None
    if tiles_per_group is not None:
        mod_spec = pl.BlockSpec((None, 1, d), lambda i: (i // tiles_per_group, 0, 0))
    else:
        mod_spec = pl.BlockSpec((None, tm, d), lambda i: (i, 0, 0))
    tab_spec = pl.BlockSpec((tm, LANES), lambda i: (i % pos_tiles, 0))
    row = lambda w: pl.BlockSpec((tm, w), lambda i: (i, 0))
    full = lambda a: pl.BlockSpec(a.shape, lambda i: (0,) * a.ndim)
    o32 = jax.ShapeDtypeStruct((n, width), F32)
    o16 = jax.ShapeDtypeStruct((n, width), BF16)
    return pl.pallas_call(
        _inproj_body,
        grid=(n // tm,),
        in_specs=[row(d), mod_spec, mod_spec, full(g1), tab_spec, tab_spec,
                  full(w_main), full(w_f), full(b_f)],
        out_specs=[row(width)] * 10 + [row(LANES)],
        out_shape=[o16, o32, o16, o32, o16, o16, o32, o16, o32, o16,
                   jax.ShapeDtypeStruct((n, LANES), F32)],
        compiler_params=_cparams(("parallel",)),
        name="inproj",
    )(x, sc, sh, g1, cos, sin, w_main, w_f, b_f)


def _cumsum_body(x_ref, o_ref, *, chunks_per_row):
    r = x_ref.shape[0]
    i0 = lax.broadcasted_iota(jnp.int32, (LANES, LANES), 0)
    i1 = lax.broadcasted_iota(jnp.int32, (LANES, LANES), 1)
    upper = (i0 <= i1).astype(F32)
    cum = jnp.dot(x_ref[...], upper, preferred_element_type=F32, precision=lax.Precision.HIGHEST)
    if chunks_per_row > 1:
        r0 = lax.broadcasted_iota(jnp.int32, (r, r), 0)
        r1 = lax.broadcasted_iota(jnp.int32, (r, r), 1)
        prev = ((r0 // chunks_per_row == r1 // chunks_per_row) & (r1 < r0)).astype(F32)
        tot = jnp.broadcast_to(cum[:, LANES - 1:LANES], (r, LANES))
        cum = cum + jnp.dot(prev, tot, preferred_element_type=F32, precision=lax.Precision.HIGHEST)
    o_ref[...] = cum


def _cumsum_rows(x, rows_per_block, chunks_per_row):
    n = x.shape[0]
    return pl.pallas_call(
        functools.partial(_cumsum_body, chunks_per_row=chunks_per_row),
        grid=(n // rows_per_block,),
        in_specs=[pl.BlockSpec((rows_per_block, LANES), lambda i: (i, 0))],
        out_specs=pl.BlockSpec((rows_per_block, LANES), lambda i: (i, 0)),
        out_shape=jax.ShapeDtypeStruct((n, LANES), F32),
        compiler_params=_cparams(("parallel",)),
        name="cumsum",
    )(x)


def _lambda_from(lamv, lam_init):
    a = jnp.sum(lamv[0:1, :] * lamv[1:2, :], axis=1, keepdims=True)
    b = jnp.sum(lamv[2:3, :] * lamv[3:4, :], axis=1, keepdims=True)
    return jnp.exp(a) - jnp.exp(b) + lam_init


def _diff_combine(o1, o2, lamv, gsub, lam_init):
    o = o1 - _lambda_from(lamv, lam_init) * o2
    ms = jnp.mean(o * o, axis=-1, keepdims=True)
    return (o * lax.rsqrt(ms + NORM_EPS) * gsub) * (1.0 - lam_init)


def _flash_body(qi_tab, ki_tab, *refs, fox, lam_init):
    if fox:
        q_ref, k_ref, v_ref, cum_ref, o_ref, m_sc, l_sc, acc_sc = refs
    else:
        q_ref, k_ref, v_ref, lamv_ref, gsub_ref, o_ref, m_sc, l_sc, acc_sc = refs
    p = pl.program_id(2)
    qi = qi_tab[p]
    ki = ki_tab[p]
    tq = q_ref.shape[0]
    tk = k_ref.shape[0]

    @pl.when(ki == 0)
    def _():
        m_sc[...] = jnp.full(m_sc.shape, -jnp.inf, F32)
        l_sc[...] = jnp.zeros(l_sc.shape, F32)
        acc_sc[...] = jnp.zeros(acc_sc.shape, F32)

    lane = lax.broadcasted_iota(jnp.int32, (tq, LANES), 1)
    low = lane < HEAD_DIM

    def step(masked):
        q = q_ref[...]
        k = k_ref[...]
        v = v_ref[...]
        zero = jnp.zeros_like(q)
        for c in range(2):
            qc = jnp.where(low, q, zero) if c == 0 else jnp.where(low, zero, q)
            s = lax.dot_general(qc, k, (((1,), (1,)), ((), ())), preferred_element_type=F32)
            if fox:
                s = s - cum_ref[c:c + 1, :]
            if masked:
                row = lax.broadcasted_iota(jnp.int32, (tq, tk), 0)
                col = lax.broadcasted_iota(jnp.int32, (tq, tk), 1)
                s = jnp.where(col <= row, s, NEG_BIG)
            m_prev = m_sc[c]
            m_new = jnp.maximum(m_prev, jnp.max(s, axis=1, keepdims=True))
            alpha = jnp.exp(m_prev - m_new)
            pr = jnp.exp(s - m_new)
            l_sc[c] = alpha * l_sc[c] + jnp.sum(pr, axis=1, keepdims=True)
            acc_sc[c] = alpha * acc_sc[c] + jnp.dot(pr.astype(BF16), v, preferred_element_type=F32)
            m_sc[c] = m_new

    @pl.when(ki < qi)
    def _():
        step(False)

    @pl.when(ki == qi)
    def _():
        step(True)
        o1 = acc_sc[0] / l_sc[0]
        o2 = acc_sc[1] / l_sc[1]
        if fox:
            o = jnp.where(low, o1, o2)
        else:
            o = _diff_combine(o1, o2, lamv_ref[...], gsub_ref[...], lam_init)
        o_ref[...] = o.astype(o_ref.dtype)


def _flash(q, k, v, extra, *, fox, lam_init):
    b, s, w = q.shape
    g = w // LANES
    t = min(ATT_TILE, s)
    nt = s // t
    pairs = [(i, j) for i in range(nt) for j in range(i + 1)]
    qi_tab = jnp.asarray([pq for pq, _ in pairs], jnp.int32)
    ki_tab = jnp.asarray([pk for _, pk in pairs], jnp.int32)
    q_spec = pl.BlockSpec((None, t, LANES), lambda bb, gg, p, qt, kt: (bb, qt[p], gg))
    kv_spec = pl.BlockSpec((None, t, LANES), lambda bb, gg, p, qt, kt: (bb, kt[p], gg))
    if fox:
        extra_specs = [pl.BlockSpec((None, None, 2, t), lambda bb, gg, p, qt, kt: (bb, gg, 0, kt[p]))]
    else:
        extra_specs = [pl.BlockSpec(extra[0].shape, lambda bb, gg, p, qt, kt: (0, 0)),
                       pl.BlockSpec(extra[1].shape, lambda bb, gg, p, qt, kt: (0, 0))]
    return pl.pallas_call(
        functools.partial(_flash_body, fox=fox, lam_init=lam_init),
        grid_spec=pltpu.PrefetchScalarGridSpec(
            num_scalar_prefetch=2,
            grid=(b, g, len(pairs)),
            in_specs=[q_spec, kv_spec, kv_spec] + extra_specs,
            out_specs=pl.BlockSpec((None, t, LANES), lambda bb, gg, p, qt, kt: (bb, qt[p], gg)),
            scratch_shapes=[pltpu.VMEM((2, t, 1), F32), pltpu.VMEM((2, t, 1), F32),
                            pltpu.VMEM((2, t, LANES), F32)]),
        out_shape=jax.ShapeDtypeStruct((b, s, w), BF16),
        compiler_params=_cparams(("parallel", "parallel", "arbitrary")),
        name="flash_fox" if fox else "flash_diff",
    )(qi_tab, ki_tab, q, k, v, *extra)


def _decode_body(pt_ref, *refs, n_pages_step, n_diff_heads, n_fox_heads, n_new, lam_init):
    npg = n_pages_step
    it = iter(refs)
    kd_refs = [next(it) for _ in range(npg)]
    vd_refs = [next(it) for _ in range(npg)]
    kf_refs = [next(it) for _ in range(npg)]
    vf_refs = [next(it) for _ in range(npg)]
    lf_refs = [next(it) for _ in range(npg)]
    (qd_ref, qf_ref, kdn_ref, vdn_ref, kfn_ref, vfn_ref, lfn_ref, lamv_ref, gsub_ref,
     od_ref, of_ref, qd_sc, qf_sc, md_sc, ld_sc, accd_sc, mf_sc, lf_sc, accf_sc, off_sc) = it
    j = pl.program_id(1)
    last = pl.num_programs(1) - 1
    hd, hf, t = n_diff_heads, n_fox_heads, n_new
    rows_d = hd * 2 * t
    rows_f = hf * t
    wf = hf * HEAD_DIM

    @pl.when(j == 0)
    def _():
        lane = lax.broadcasted_iota(jnp.int32, (t, LANES), 1)
        qd = qd_ref[...]
        for h in range(hd):
            blk = qd[:, h * LANES:(h + 1) * LANES]
            zero = jnp.zeros_like(blk)
            qd_sc[(2 * h) * t:(2 * h + 1) * t, :] = jnp.where(lane < HEAD_DIM, blk, zero)
            qd_sc[(2 * h + 1) * t:(2 * h + 2) * t, :] = jnp.where(lane < HEAD_DIM, zero, blk)
        qf = qf_ref[...]
        lane_f = lax.broadcasted_iota(jnp.int32, (t, wf), 1)
        for h in range(hf):
            qf_sc[h * t:(h + 1) * t, :] = jnp.where(lane_f // HEAD_DIM == h, qf, jnp.zeros_like(qf))
        md_sc[...] = jnp.full(md_sc.shape, -jnp.inf, F32)
        ld_sc[...] = jnp.zeros(ld_sc.shape, F32)
        accd_sc[...] = jnp.zeros(accd_sc.shape, F32)
        mf_sc[...] = jnp.full(mf_sc.shape, -jnp.inf, F32)
        lf_sc[...] = jnp.zeros(lf_sc.shape, F32)
        accf_sc[...] = jnp.zeros(accf_sc.shape, F32)
        off_sc[...] = jnp.zeros(off_sc.shape, F32)

    def online(s, m_sc, l_sc, acc_sc, pv):
        m_prev = m_sc[...]
        m_new = jnp.maximum(m_prev, jnp.max(s, axis=1, keepdims=True))
        alpha = jnp.exp(m_prev - m_new)
        pr = jnp.exp(s - m_new)
        l_sc[...] = alpha * l_sc[...] + jnp.sum(pr, axis=1, keepdims=True)
        acc_sc[...] = alpha * acc_sc[...] + pv(pr.astype(BF16))
        m_sc[...] = m_new

    def diff_page(k_rows, v_rows, causal):
        nr = k_rows.shape[0]
        kb = k_rows.astype(BF16)
        vb = v_rows.astype(BF16)
        s = lax.dot_general(qd_sc[...], kb, (((1,), (1,)), ((), ())), preferred_element_type=F32)
        r = lax.broadcasted_iota(jnp.int32, (rows_d, nr), 0)
        c = lax.broadcasted_iota(jnp.int32, (rows_d, nr), 1)
        ok = (c % hd) == (r // (2 * t))
        if causal:
            ok = ok & ((c // hd) <= (r % t))
        s = jnp.where(ok, s, NEG_BIG)
        online(s, md_sc, ld_sc, accd_sc, lambda pb: jnp.dot(pb, vb, preferred_element_type=F32))

    def fox_page(kt, vt, cum_in, causal):
        nt = kt.shape[1]
        kb = kt.astype(BF16)
        vb = vt.astype(BF16)
        s = jnp.dot(qf_sc[...], kb, preferred_element_type=F32)
        cum = cum_in + off_sc[...]
        bias = jnp.concatenate([jnp.broadcast_to(cum[h:h + 1, :], (t, nt)) for h in range(hf)], axis=0)
        s = s - bias
        if causal:
            r = lax.broadcasted_iota(jnp.int32, (rows_f, nt), 0)
            c = lax.broadcasted_iota(jnp.int32, (rows_f, nt), 1)
            s = jnp.where(c <= (r % t), s, NEG_BIG)
        online(s, mf_sc, lf_sc, accf_sc,
               lambda pb: lax.dot_general(pb, vb, (((1,), (1,)), ((), ())), preferred_element_type=F32))
        off_sc[...] = off_sc[...] + cum_in[:, nt - 1:nt]

    @pl.when(j < last)
    def _():
        for i in range(npg):
            diff_page(kd_refs[i][...], vd_refs[i][...], False)
            fox_page(kf_refs[i][...], vf_refs[i][...], lf_refs[i][...], False)

    @pl.when(j == last)
    def _():
        diff_page(kdn_ref[...], vdn_ref[...], True)
        fox_page(kfn_ref[...], vfn_ref[...], lfn_ref[...], True)
        od = accd_sc[...] / ld_sc[...]
        for h in range(hd):
            o1 = od[(2 * h) * t:(2 * h + 1) * t, :]
            o2 = od[(2 * h + 1) * t:(2 * h + 2) * t, :]
            o = _diff_combine(o1, o2, lamv_ref[...], gsub_ref[...], lam_init)
            od_ref[:, h * LANES:(h + 1) * LANES] = o.astype(od_ref.dtype)
        of_all = accf_sc[...] / lf_sc[...]
        lane_f = lax.broadcasted_iota(jnp.int32, (t, wf), 1)
        acc = jnp.zeros((t, wf), F32)
        for h in range(hf):
            acc = acc + jnp.where(lane_f // HEAD_DIM == h, of_all[h * t:(h + 1) * t, :], 0.0)
        of_ref[...] = acc.astype(of_ref.dtype)


def _decode(page_table, ckd, cvd, ckf, cvf, clf, qd, qf, kdn, vdn, kfn, vfn, lfn, lamv, gsub, lam_init):
    b, n_pages = page_table.shape
    t, wd = qd.shape[1], qd.shape[2]
    wf = qf.shape[2]
    hd = wd // LANES
    hf = wf // HEAD_DIM
    npg = math.gcd(PAGES_PER_STEP, n_pages)
    steps = n_pages // npg

    def page_spec(arr, i):
        def imap(bb, jj, pt):
            return (pt[bb, jnp.minimum(jj, steps - 1) * npg + i], 0, 0)
        return pl.BlockSpec((None,) + arr.shape[1:], imap)

    per_b = lambda arr: pl.BlockSpec((None,) + arr.shape[1:], lambda bb, jj, pt: (bb, 0, 0))
    full = lambda arr: pl.BlockSpec(arr.shape, lambda bb, jj, pt: (0, 0))
    in_specs = []
    args = []
    for arr in (ckd, cvd, ckf, cvf, clf):
        for i in range(npg):
            in_specs.append(page_spec(arr, i))
            args.append(arr)
    for arr in (qd, qf, kdn, vdn, kfn, vfn, lfn):
        in_specs.append(per_b(arr))
        args.append(arr)
    in_specs += [full(lamv), full(gsub)]
    args += [lamv, gsub]
    rows_d = hd * 2 * t
    rows_f = hf * t
    return pl.pallas_call(
        functools.partial(_decode_body, n_pages_step=npg, n_diff_heads=hd, n_fox_heads=hf, n_new=t,
                          lam_init=lam_init),
        grid_spec=pltpu.PrefetchScalarGridSpec(
            num_scalar_prefetch=1,
            grid=(b, steps + 1),
            in_specs=in_specs,
            out_specs=[pl.BlockSpec((None, t, wd), lambda bb, jj, pt: (bb, 0, 0)),
                       pl.BlockSpec((None, t, wf), lambda bb, jj, pt: (bb, 0, 0))],
            scratch_shapes=[pltpu.VMEM((rows_d, LANES), BF16), pltpu.VMEM((rows_f, wf), BF16),
                            pltpu.VMEM((rows_d, 1), F32), pltpu.VMEM((rows_d, 1), F32),
                            pltpu.VMEM((rows_d, LANES), F32),
                            pltpu.VMEM((rows_f, 1), F32), pltpu.VMEM((rows_f, 1), F32),
                            pltpu.VMEM((rows_f, wf), F32),
                            pltpu.VMEM((hf, 1), F32)]),
        out_shape=[jax.ShapeDtypeStruct((b, t, wd), BF16), jax.ShapeDtypeStruct((b, t, wf), BF16)],
        compiler_params=_cparams(("parallel", "arbitrary")),
        name="decode",
    )(page_table, *args)


def _outproj_body(od_ref, of_ref, x_ref, ga_ref, sc_ref, sh_ref, g_ref, wo_ref, wr_ref, br_ref,
                  x1_ref, h_ref, idx_ref, gate_ref, *, n_experts):
    tm = x_ref.shape[0]
    wd = od_ref.shape[1]
    o = (jnp.dot(od_ref[...], wo_ref[:wd, :], preferred_element_type=F32)
         + jnp.dot(of_ref[...], wo_ref[wd:, :], preferred_element_type=F32))
    x1 = x_ref[...] + ga_ref[...] * o
    x1_ref[...] = x1
    h = _rms_mod(x1, g_ref[...], sc_ref[...], sh_ref[...])
    h_ref[...] = h
    logits = jnp.dot(h, wr_ref[...], preferred_element_type=F32, precision=lax.Precision.HIGHEST) + br_ref[...]
    lane = lax.broadcasted_iota(jnp.int32, (tm, LANES), 1).astype(F32)
    cur = jnp.where(lane < n_experts, logits, -jnp.inf)
    idx_out = jnp.zeros((tm, LANES), F32)
    val_out = jnp.zeros((tm, LANES), F32)
    top = None
    den = jnp.zeros((tm, 1), F32)
    for k in range(TOP_K):
        mx = jnp.max(cur, axis=1, keepdims=True)
        sel = jnp.min(jnp.where(cur == mx, lane, float(LANES)), axis=1, keepdims=True)
        if k == 0:
            top = mx
        e = jnp.exp(mx - top)
        den = den + e
        idx_out = jnp.where(lane == k, sel, idx_out)
        val_out = jnp.where(lane == k, e, val_out)
        cur = jnp.where(lane == sel, -jnp.inf, cur)
    idx_ref[...] = idx_out.astype(jnp.int32)
    gate_ref[...] = val_out / den


def _outproj(od, of, x, ga, sc, sh, g2, w_out, w_r, b_r, rows_per_group, n_experts):
    n, d = x.shape
    tm = ROW_TILE
    tiles_per_group = rows_per_group // tm if rows_per_group >= tm else None
    if tiles_per_group is not None:
        mod_spec = pl.BlockSpec((None, 1, d), lambda i: (i // tiles_per_group, 0, 0))
    else:
        mod_spec = pl.BlockSpec((None, tm, d), lambda i: (i, 0, 0))
    row = lambda w: pl.BlockSpec((tm, w), lambda i: (i, 0))
    full = lambda a: pl.BlockSpec(a.shape, lambda i: (0,) * a.ndim)
    return pl.pallas_call(
        functools.partial(_outproj_body, n_experts=n_experts),
        grid=(n // tm,),
        in_specs=[row(od.shape[1]), row(of.shape[1]), row(d), mod_spec, mod_spec, mod_spec,
                  full(g2), full(w_out), full(w_r), full(b_r)],
        out_specs=[row(d), row(d), row(LANES), row(LANES)],
        out_shape=[jax.ShapeDtypeStruct((n, d), F32), jax.ShapeDtypeStruct((n, d), F32),
                   jax.ShapeDtypeStruct((n, LANES), jnp.int32), jax.ShapeDtypeStruct((n, LANES), F32)],
        compiler_params=_cparams(("parallel",)),
        name="outproj",
    )(od, of, x, ga, sc, sh, g2, w_out, w_r, b_r)


def _moe_body(be_ref, tok_next_ref, tok_first_ref, dst_ref, gate_ref, h_hbm, wgu_ref, bgu_ref, wd_ref, bd_ref,
              y_hbm, tok_sm, dst_sm, xbuf, ybuf, sem_idx, sem_g, sem_s):
    del be_ref
    i = pl.program_id(0)
    n = pl.num_programs(0)
    rows = xbuf.shape[1]
    slot = i % 2
    nslot = 1 - slot

    def gather_block(buf_slot):
        def issue(r, carry):
            pltpu.make_async_copy(h_hbm.at[pl.ds(tok_sm[0, r], 1), :],
                                  xbuf.at[buf_slot, pl.ds(r, 1), :], sem_g.at[buf_slot]).start()
            return carry
        lax.fori_loop(0, rows, issue, 0, unroll=8)

    def load_indices(src_ref, dst_smem):
        cp = pltpu.make_async_copy(src_ref.at[0], dst_smem, sem_idx)
        cp.start()
        cp.wait()

    @pl.when(i == 0)
    def _():
        load_indices(tok_first_ref, tok_sm)
        gather_block(0)

    @pl.when(i + 1 < n)
    def _():
        load_indices(tok_next_ref, tok_sm)
        gather_block(nslot)

    pltpu.make_async_copy(h_hbm.at[pl.ds(0, rows), :], xbuf.at[slot], sem_g.at[slot]).wait()

    @pl.when(i >= 2)
    def _():
        pltpu.make_async_copy(ybuf.at[slot], y_hbm.at[pl.ds(0, rows), :], sem_s.at[slot]).wait()

    x = xbuf[slot].astype(BF16)
    gu = jnp.dot(x, wgu_ref[...], preferred_element_type=F32) + bgu_ref[...]
    dff = gu.shape[1] // 2
    g = jnp.minimum(gu[:, :dff], SWIGLU_LIMIT)
    u = jnp.clip(gu[:, dff:], -SWIGLU_LIMIT, SWIGLU_LIMIT)
    act = (u + 1.0) * g * jax.nn.sigmoid(SWIGLU_ALPHA * g)
    y = jnp.dot(act.astype(BF16), wd_ref[...], preferred_element_type=F32) + bd_ref[...]
    ybuf[slot] = y * gate_ref[...]

    load_indices(dst_ref, dst_sm)

    def scatter(r, carry):
        pltpu.make_async_copy(ybuf.at[slot, pl.ds(r, 1), :],
                              y_hbm.at[pl.ds(dst_sm[0, r], 1), :], sem_s.at[slot]).start()
        return carry
    lax.fori_loop(0, rows, scatter, 0, unroll=8)

    @pl.when(i == n - 1)
    def _():
        pltpu.make_async_copy(ybuf.at[slot], y_hbm.at[pl.ds(0, rows), :], sem_s.at[slot]).wait()

        @pl.when(n >= 2)
        def _():
            pltpu.make_async_copy(ybuf.at[nslot], y_hbm.at[pl.ds(0, rows), :], sem_s.at[nslot]).wait()


def _moe(h, idx, gates, w_gu, b_gu, w_down, b_down):
    n, d = h.shape
    n_experts = w_gu.shape[0]
    blk = EXPERT_ROWS
    n_asg = n * TOP_K
    flat_e = idx.reshape(-1)
    order = jnp.argsort(flat_e)
    e_sorted = flat_e[order]
    counts = jnp.bincount(flat_e, length=n_experts)
    padded = (counts + blk - 1) // blk * blk
    pad_end = jnp.cumsum(padded)
    pad_start = pad_end - padded
    start = jnp.cumsum(counts) - counts
    slot = (pad_start[e_sorted] + jnp.arange(n_asg) - start[e_sorted]).astype(jnp.int32)
    n_blocks = -(-n_asg // blk) + n_experts
    n_slots = n_blocks * blk
    n_pad = n_slots - n_asg
    tok_sorted = (order // TOP_K).astype(jnp.int32)
    dst_sorted = ((order % TOP_K) * n + order // TOP_K).astype(jnp.int32)
    used = jnp.zeros((n_slots,), jnp.int32).at[slot].set(1)
    pad_rank = jnp.cumsum(1 - used) - 1
    slot_tok = jnp.zeros((n_slots,), jnp.int32).at[slot].set(tok_sorted)
    slot_dst = (n_asg + pad_rank).astype(jnp.int32).at[slot].set(dst_sorted)
    slot_gate = jnp.zeros((n_slots,), F32).at[slot].set(gates.reshape(-1)[order])
    block_expert = jnp.minimum(
        jnp.searchsorted(pad_end, jnp.arange(n_blocks) * blk, side='right'), n_experts - 1).astype(jnp.int32)
    slot_tok = slot_tok.reshape(n_blocks, 1, blk)
    slot_dst = slot_dst.reshape(n_blocks, 1, blk)

    y = pl.pallas_call(
        _moe_body,
        grid_spec=pltpu.PrefetchScalarGridSpec(
            num_scalar_prefetch=1,
            grid=(n_blocks,),
            in_specs=[
                pl.BlockSpec((1, 1, blk), lambda i, be: (jnp.minimum(i + 1, n_blocks - 1), 0, 0)),
                pl.BlockSpec((1, 1, blk), lambda i, be: (0, 0, 0)),
                pl.BlockSpec((1, 1, blk), lambda i, be: (i, 0, 0)),
                pl.BlockSpec((blk, 1), lambda i, be: (i, 0)),
                pl.BlockSpec(memory_space=pl.ANY),
                pl.BlockSpec((None, d, w_gu.shape[2]), lambda i, be: (be[i], 0, 0)),
                pl.BlockSpec((None, 1, b_gu.shape[2]), lambda i, be: (be[i], 0, 0)),
                pl.BlockSpec((None, w_down.shape[1], d), lambda i, be: (be[i], 0, 0)),
                pl.BlockSpec((None, 1, d), lambda i, be: (be[i], 0, 0)),
            ],
            out_specs=pl.BlockSpec(memory_space=pl.ANY),
            scratch_shapes=[pltpu.SMEM((1, blk), jnp.int32), pltpu.SMEM((1, blk), jnp.int32),
                            pltpu.VMEM((2, blk, d), F32), pltpu.VMEM((2, blk, d), F32),
                            pltpu.SemaphoreType.DMA, pltpu.SemaphoreType.DMA((2,)),
                            pltpu.SemaphoreType.DMA((2,))]),
        out_shape=jax.ShapeDtypeStruct((n_asg + n_pad, d), F32),
        compiler_params=_cparams(("arbitrary",)),
        name="moe",
    )(block_expert, slot_tok, slot_tok, slot_dst, slot_gate.reshape(n_slots, 1), h, w_gu, b_gu, w_down, b_down)
    return y


def _final_body(x_ref, ga_ref, y0_ref, y1_ref, y2_ref, y3_ref, g_ref, o_ref):
    moe = (y0_ref[...] + y1_ref[...]) + (y2_ref[...] + y3_ref[...])
    x = x_ref[...] + ga_ref[...] * moe
    ms = jnp.mean(x * x, axis=-1, keepdims=True)
    o_ref[...] = x * lax.rsqrt(ms + NORM_EPS) * g_ref[...]


def _final(x1, ga, y_un, g_final, rows_per_group):
    n, d = x1.shape
    tm = ROW_TILE
    nt = n // tm
    tiles_per_group = rows_per_group // tm if rows_per_group >= tm else None
    if tiles_per_group is not None:
        mod_spec = pl.BlockSpec((None, 1, d), lambda i: (i // tiles_per_group, 0, 0))
    else:
        mod_spec = pl.BlockSpec((None, tm, d), lambda i: (i, 0, 0))
    yk = lambda k: pl.BlockSpec((tm, d), lambda i: (k * nt + i, 0))
    return pl.pallas_call(
        _final_body,
        grid=(nt,),
        in_specs=[pl.BlockSpec((tm, d), lambda i: (i, 0)), mod_spec,
                  yk(0), yk(1), yk(2), yk(3), pl.BlockSpec((1, d), lambda i: (0, 0))],
        out_specs=pl.BlockSpec((tm, d), lambda i: (i, 0)),
        out_shape=jax.ShapeDtypeStruct((n, d), F32),
        compiler_params=_cparams(("parallel",)),
        name="final",
    )(x1, ga, y_un, y_un, y_un, y_un, g_final)


def _rope_tables(pos):
    half = HEAD_DIM // 2
    inv = 1.0 / (ROPE_THETA ** (jnp.arange(half, dtype=F32) / half))
    ang = pos.astype(F32)[:, None] * inv[None, :]
    cos = jnp.tile(jnp.cos(ang), (1, LANES // half))
    sin = jnp.sin(ang)
    sin = jnp.tile(jnp.concatenate([-sin, sin], axis=1), (1, LANES // HEAD_DIM))
    return cos, sin


def _mods_for(m, rows_per_group):
    g = m.shape[0]
    parts = jnp.split(m, 6, axis=-1)
    if rows_per_group >= ROW_TILE:
        return [p[:, None, :] for p in parts]
    groups_per_tile = ROW_TILE // rows_per_group
    return [jnp.repeat(p, rows_per_group, axis=0).reshape(g // groups_per_tile, ROW_TILE, -1) for p in parts]


def kernel(x_prompt, x_sample, c_prompt, c_sample, cache_k_diff, cache_v_diff, cache_k_fox, cache_v_fox, cache_logf_fox, page_table, w_ada, b_ada, g_norm1, g_norm2, w_in, b_fgate, lam_q1, lam_k1, lam_q2, lam_k2, g_subln, w_out, w_router, b_router, w_gu, b_gu, w_down, b_down, g_final):
    bp, s, d = x_prompt.shape
    bs, t, _ = x_sample.shape
    depth = w_ada.shape[0]
    assert depth == 1
    l = 0
    lam_init = 0.8 - 0.6 * math.exp(-0.3 * l)
    n_pool, page = cache_k_diff.shape[1], cache_k_diff.shape[2]
    hd = cache_k_diff.shape[3]
    hf = cache_k_fox.shape[3]
    wd = hd * 2 * HEAD_DIM
    wf = hf * HEAD_DIM
    n_experts = w_router.shape[2]
    past = page_table.shape[1] * page

    scale = HEAD_DIM ** -0.5
    w_in_l = w_in[l]
    col_scale = jnp.concatenate([jnp.full((wd,), scale, F32), jnp.ones((2 * wd,), F32),
                                 jnp.full((wf,), scale, F32), jnp.ones((2 * wf,), F32)])
    w_main = (w_in_l[:, :3 * wd + 3 * wf] * col_scale[None, :]).astype(BF16)
    w_f = jnp.pad(w_in_l[:, 3 * wd + 3 * wf:], ((0, 0), (0, LANES - hf))).astype(BF16)
    b_f = jnp.pad(b_fgate[l], (0, LANES - hf)).reshape(1, LANES)
    w_out_b = w_out[l].astype(BF16)
    w_r = jnp.pad(w_router[l], ((0, 0), (0, LANES - n_experts)))
    b_r = jnp.pad(b_router[l], (0, LANES - n_experts)).reshape(1, LANES)
    w_gu_b = w_gu[l].astype(BF16)
    w_down_b = w_down[l].astype(BF16)
    b_gu_l = b_gu[l][:, None, :]
    b_down_l = b_down[l][:, None, :]
    g1 = g_norm1[l].reshape(1, d)
    g2 = g_norm2[l].reshape(1, d)
    gf = g_final.reshape(1, d)
    lamv = jnp.stack([lam_q1[l], lam_k1[l], lam_q2[l], lam_k2[l]], axis=0)
    gsub = g_subln[l].reshape(1, 2 * HEAD_DIM)

    mods = _adaln(jnp.concatenate([c_prompt, c_sample], axis=0), w_ada[l], b_ada[l])
    sh1p, sc1p, ga1p, sh2p, sc2p, ga2p = _mods_for(mods[:bp], s)
    sh1s, sc1s, ga1s, sh2s, sc2s, ga2s = _mods_for(mods[bp:], t)

    cos_p, sin_p = _rope_tables(jnp.arange(s))
    xp = x_prompt.reshape(bp * s, d)
    (qd, kd32, kd16, vd32, vd16, qf, kf32, kf16, vf32, vf16, lfp) = _inproj(
        xp, sc1p, sh1p, g1, cos_p, sin_p, w_main, w_f, b_f, s, s // ROW_TILE)
    logf_p = lfp[:, :hf]
    chunks = s // LANES
    lf_rows = jnp.transpose(logf_p.reshape(bp, s, hf), (0, 2, 1)).reshape(bp * hf * chunks, LANES)
    cum = _cumsum_rows(lf_rows, hf * chunks, chunks).reshape(bp, hf // 2, 2, s)
    b3 = lambda a: a.reshape(bp, s, -1)
    od_p = _flash(b3(qd), b3(kd16), b3(vd16), (lamv, gsub), fox=False, lam_init=lam_init)
    of_p = _flash(b3(qf), b3(kf16), b3(vf16), (cum,), fox=True, lam_init=lam_init)
    x1p, h2p, idxp, gatep = _outproj(od_p.reshape(bp * s, wd), of_p.reshape(bp * s, wf), xp,
                                     ga1p, sc2p, sh2p, g2, w_out_b, w_r, b_r, s, n_experts)
    yp = _moe(h2p, idxp[:, :TOP_K], gatep[:, :TOP_K], w_gu_b, b_gu_l, w_down_b, b_down_l)
    y_prompt = _final(x1p, ga2p, yp, gf, s).reshape(bp, s, d)

    cos_s, sin_s = _rope_tables(past + jnp.arange(t))
    reps = ROW_TILE // t
    cos_s = jnp.tile(cos_s, (reps, 1))
    sin_s = jnp.tile(sin_s, (reps, 1))
    xs = x_sample.reshape(bs * t, d)
    (qds, kd32s, _u0, vd32s, _u1, qfs, kf32s, _u2, vf32s, _u3, lfs) = _inproj(
        xs, sc1s, sh1s, g1, cos_s, sin_s, w_main, w_f, b_f, t, 1)
    logf_s = lfs[:, :hf]
    ckd = cache_k_diff[l].reshape(n_pool, page * hd, 2 * HEAD_DIM)
    cvd = cache_v_diff[l].reshape(n_pool, page * hd, 2 * HEAD_DIM)
    ckf = jnp.transpose(cache_k_fox[l], (0, 2, 3, 1)).reshape(n_pool, wf, page)
    cvf = jnp.transpose(cache_v_fox[l], (0, 2, 3, 1)).reshape(n_pool, wf, page)
    clf = jnp.transpose(cache_logf_fox[l], (0, 2, 1)).reshape(n_pool * hf, page)
    clf_cum = _cumsum_rows(clf, math.gcd(2048, n_pool * hf), 1).reshape(n_pool, hf, page)
    tp = 16
    padt = lambda a: jnp.pad(a.reshape(bs, t, -1), ((0, 0), (0, tp - t), (0, 0)))
    kdn = padt(kd32s).reshape(bs, tp * hd, 2 * HEAD_DIM)
    vdn = padt(vd32s).reshape(bs, tp * hd, 2 * HEAD_DIM)
    padl = lambda a: jnp.pad(jnp.transpose(a.reshape(bs, t, -1), (0, 2, 1)), ((0, 0), (0, 0), (0, LANES - t)))
    kfn = padl(kf32s)
    vfn = padl(vf32s)
    lfn = _cumsum_rows(padl(logf_s).reshape(bs * hf, LANES), bs * hf, 1).reshape(bs, hf, LANES)
    od_s, of_s = _decode(page_table, ckd, cvd, ckf, cvf, clf_cum, qds.reshape(bs, t, wd), qfs.reshape(bs, t, wf),
                         kdn, vdn, kfn, vfn, lfn, lamv, gsub, lam_init)
    x1s, h2s, idxs, gates = _outproj(od_s.reshape(bs * t, wd), of_s.reshape(bs * t, wf), xs,
                                     ga1s, sc2s, sh2s, g2, w_out_b, w_r, b_r, t, n_experts)
    ys = _moe(h2s, idxs[:, :TOP_K], gates[:, :TOP_K], w_gu_b, b_gu_l, w_down_b, b_down_l)
    y_sample = _final(x1s, ga2s, ys, gf, t).reshape(bs, t, d)

    rp = lambda a, h_, c_: a.reshape(1, bp, s, h_, c_)
    rs = lambda a, h_, c_: a.reshape(1, bs, t, h_, c_)
    return (y_prompt, y_sample,
            rp(kd32, hd, 2 * HEAD_DIM), rp(vd32, hd, 2 * HEAD_DIM), rp(kf32, hf, HEAD_DIM), rp(vf32, hf, HEAD_DIM),
            logf_p.reshape(1, bp, s, hf),
            rs(kd32s, hd, 2 * HEAD_DIM), rs(vd32s, hd, 2 * HEAD_DIM), rs(kf32s, hf, HEAD_DIM), rs(vf32s, hf, HEAD_DIM),
            logf_s.reshape(1, bs, t, hf))
```

```python
import functools
import math

import jax
import jax.numpy as jnp
import numpy as np
from jax import lax
from jax.experimental import pallas as pl
from jax.experimental.pallas import tpu as pltpu

HEAD_DIM = 64
LANES = 128
ROPE_THETA = 10000.0
NORM_EPS = 1e-5
TOP_K = 4
SWIGLU_ALPHA = 1.702
SWIGLU_LIMIT = 7.0
NEG_BIG = -1e30
ROW_TILE = 512
ATT_TILE = 1024
ATT_SUB_Q = 256
ATT_SUB_K = 512
EXPERT_ROWS = 256
PAGES_PER_STEP = 8
VMEM_LIMIT = 56 * 1024 * 1024

F32 = jnp.float32
BF16 = jnp.bfloat16


def _cparams(sem):
    return pltpu.CompilerParams(dimension_semantics=sem, vmem_limit_bytes=VMEM_LIMIT)


def _store_row_tiles(ref, val, lead=()):
    rows, d = val.shape
    k = d // LANES
    for j in range(k):
        ref[lead + (pl.ds(j, rows, stride=k), slice(None))] = val[:, j * LANES:(j + 1) * LANES]


def _load_row_tiles(ref, rows, d, lead=()):
    k = d // LANES
    return jnp.concatenate([ref[lead + (pl.ds(j, rows, stride=k), slice(None))] for j in range(k)], axis=1)


def _adaln_body(c_ref, w_ref, b_ref, o_ref):
    c = c_ref[...]
    s = (c * jax.nn.sigmoid(c)).astype(BF16)
    o_ref[...] = jnp.dot(s, w_ref[...].astype(BF16), preferred_element_type=F32) + b_ref[...]


def _adaln(c, w, b):
    n, d = c.shape
    nout = w.shape[1]
    tn = 1024
    return pl.pallas_call(
        _adaln_body,
        grid=(nout // tn,),
        in_specs=[pl.BlockSpec((n, d), lambda j: (0, 0)),
                  pl.BlockSpec((d, tn), lambda j: (0, j)),
                  pl.BlockSpec((1, tn), lambda j: (0, j))],
        out_specs=pl.BlockSpec((n, tn), lambda j: (0, j)),
        out_shape=jax.ShapeDtypeStruct((n, nout), F32),
        compiler_params=_cparams(("arbitrary",)),
        name="adaln",
    )(c, w, b.reshape(1, nout))


def _rms_mod(x, g, sc, sh):
    ms = jnp.mean(x * x, axis=-1, keepdims=True)
    return (x * lax.rsqrt(ms + NORM_EPS) * g) * (1.0 + sc) + sh


def _inproj_body(x_ref, sc_ref, sh_ref, g_ref, cos_ref, sin_ref, w_ref, wf_ref, bf_ref,
                 qd_ref, kd32_ref, kd16_ref, vd32_ref, vd16_ref,
                 qf_ref, kf32_ref, kf16_ref, vf32_ref, vf16_ref, lf_ref, *, fox_transposed):
    tm = x_ref.shape[0]
    hb = _rms_mod(x_ref[...], g_ref[...], sc_ref[...], sh_ref[...]).astype(BF16)
    cos = cos_ref[...]
    sin = sin_ref[...]
    lane = lax.broadcasted_iota(jnp.int32, (tm, LANES), 1)
    first_half = (lane % HEAD_DIM) < (HEAD_DIM // 2)

    def rope(blk):
        partner = jnp.where(first_half, pltpu.roll(blk, LANES - HEAD_DIM // 2, 1),
                            pltpu.roll(blk, HEAD_DIM // 2, 1))
        return blk * cos + partner * sin

    width = qd_ref.shape[1]
    nblk = width // LANES
    outs = ((qd_ref,), (kd32_ref, kd16_ref), (vd32_ref, vd16_ref),
            (qf_ref,), (kf32_ref, kf16_ref), (vf32_ref, vf16_ref))
    for gidx, refs in enumerate(outs):
        z = jnp.dot(hb, w_ref[:, gidx * width:(gidx + 1) * width], preferred_element_type=F32)
        if fox_transposed and gidx >= 4:
            refs[0][...] = z.T
            refs = refs[1:]
        for j in range(nblk):
            blk = z[:, j * LANES:(j + 1) * LANES]
            if gidx < 2:
                blk = rope(blk)
            for r in refs:
                if r is kd32_ref or r is vd32_ref:
                    r[pl.ds(j, tm, stride=nblk), :] = blk
                else:
                    r[:, j * LANES:(j + 1) * LANES] = blk.astype(r.dtype)
    fl = jnp.dot(hb, wf_ref[...], preferred_element_type=F32) + bf_ref[...]
    lf_ref[...] = jax.nn.log_sigmoid(fl)


def _inproj(x, sc, sh, g1, cos, sin, w_main, w_f, b_f, rows_per_group, pos_tiles):
    n, d = x.shape
    tm = ROW_TILE
    width = w_main.shape[1] // 6
    tiles_per_group = rows_per_group // tm if rows_per_group >= tm else None
    if tiles_per_group is not None:
        mod_spec = pl.BlockSpec((None, 1, d), lambda i: (i // tiles_per_group, 0, 0))
    else:
        mod_spec = pl.BlockSpec((None, tm, d), lambda i: (i, 0, 0))
    tab_spec = pl.BlockSpec((tm, LANES), lambda i: (i % pos_tiles, 0))
    row = lambda w: pl.BlockSpec((tm, w), lambda i: (i, 0))
    full = lambda a: pl.BlockSpec(a.shape, lambda i: (0,) * a.ndim)
    o16 = jax.ShapeDtypeStruct((n, width), BF16)
    nblk = width // LANES
    ort = jax.ShapeDtypeStruct((n * nblk, LANES), F32)
    rt_spec = pl.BlockSpec((tm * nblk, LANES), lambda i: (i, 0))
    fox_transposed = tiles_per_group is not None
    if fox_transposed:
        ofx = jax.ShapeDtypeStruct((n // rows_per_group, width, rows_per_group), F32)
        fx_spec = pl.BlockSpec((None, width, tm), lambda i: (i // tiles_per_group, 0, i % tiles_per_group))
    else:
        ofx = jax.ShapeDtypeStruct((n, width), F32)
        fx_spec = row(width)
    return pl.pallas_call(
        functools.partial(_inproj_body, fox_transposed=fox_transposed),
        grid=(n // tm,),
        in_specs=[row(d), mod_spec, mod_spec, full(g1), tab_spec, tab_spec,
                  full(w_main), full(w_f), full(b_f)],
        out_specs=[row(width), rt_spec, row(width), rt_spec, row(width),
                   row(width), fx_spec, row(width), fx_spec, row(width), row(LANES)],
        out_shape=[o16, ort, o16, ort, o16, o16, ofx, o16, ofx, o16,
                   jax.ShapeDtypeStruct((n, LANES), F32)],
        compiler_params=_cparams(("parallel",)),
        name="inproj",
    )(x, sc, sh, g1, cos, sin, w_main, w_f, b_f)


def _cumsum_body(x_ref, o_ref, *, chunks_per_row):
    r = x_ref.shape[0]
    i0 = lax.broadcasted_iota(jnp.int32, (LANES, LANES), 0)
    i1 = lax.broadcasted_iota(jnp.int32, (LANES, LANES), 1)
    upper = (i0 <= i1).astype(F32)
    cum = jnp.dot(x_ref[...], upper, preferred_element_type=F32, precision=lax.Precision.HIGHEST)
    if chunks_per_row > 1:
        r0 = lax.broadcasted_iota(jnp.int32, (r, r), 0)
        r1 = lax.broadcasted_iota(jnp.int32, (r, r), 1)
        prev = ((r0 // chunks_per_row == r1 // chunks_per_row) & (r1 < r0)).astype(F32)
        tot = jnp.broadcast_to(cum[:, LANES - 1:LANES], (r, LANES))
        cum = cum + jnp.dot(prev, tot, preferred_element_type=F32, precision=lax.Precision.HIGHEST)
    o_ref[...] = cum


def _cumsum_rows(x, rows_per_block, chunks_per_row):
    n = x.shape[0]
    return pl.pallas_call(
        functools.partial(_cumsum_body, chunks_per_row=chunks_per_row),
        grid=(n // rows_per_block,),
        in_specs=[pl.BlockSpec((rows_per_block, LANES), lambda i: (i, 0))],
        out_specs=pl.BlockSpec((rows_per_block, LANES), lambda i: (i, 0)),
        out_shape=jax.ShapeDtypeStruct((n, LANES), F32),
        compiler_params=_cparams(("parallel",)),
        name="cumsum",
    )(x)


def _lambda_from(lamv, lam_init):
    a = jnp.sum(lamv[0:1, :] * lamv[1:2, :], axis=1, keepdims=True)
    b = jnp.sum(lamv[2:3, :] * lamv[3:4, :], axis=1, keepdims=True)
    return jnp.exp(a) - jnp.exp(b) + lam_init


def _diff_combine(o1, o2, lamv, gsub, lam_init):
    o = o1 - _lambda_from(lamv, lam_init) * o2
    ms = jnp.mean(o * o, axis=-1, keepdims=True)
    return (o * lax.rsqrt(ms + NORM_EPS) * gsub) * (1.0 - lam_init)


def _flash_body(qi_tab, ki_tab, *refs, fox, lam_init):
    if fox:
        q_ref, k_ref, v_ref, cum_ref, o_ref, m_sc, l_sc, acc_sc = refs
    else:
        q_ref, k_ref, v_ref, lamv_ref, gsub_ref, o_ref, m_sc, l_sc, acc_sc = refs
    p = pl.program_id(2)
    qi = qi_tab[p]
    ki = ki_tab[p]
    tq = q_ref.shape[0]
    tk = k_ref.shape[0]

    @pl.when(ki == 0)
    def _():
        m_sc[...] = jnp.full(m_sc.shape, -jnp.inf, F32)
        l_sc[...] = jnp.zeros(l_sc.shape, F32)
        acc_sc[...] = jnp.zeros(acc_sc.shape, F32)

    lane = lax.broadcasted_iota(jnp.int32, (tq, LANES), 1)
    low = lane < HEAD_DIM

    sq = min(ATT_SUB_Q, tq)
    sk = min(ATT_SUB_K, tk)

    def step(masked):
        nq = tq // sq
        lo = lax.broadcasted_iota(jnp.int32, (sq, LANES), 1) < HEAD_DIM
        state = {}
        qcs = {}
        for qb in range(nq):
            qs = slice(qb * sq, (qb + 1) * sq)
            q = q_ref[qs, :]
            zero = jnp.zeros_like(q)
            for c in range(2):
                qcs[qb, c] = jnp.where(lo, q, zero) if c == 0 else jnp.where(lo, zero, q)
                state[qb, c] = (m_sc[c, qs, :], l_sc[c, qs, :], acc_sc[c, qs, :])
        pieces = [(kb, qb, c) for kb in range(tk // sk) for qb in range(nq) for c in range(2)
                  if not (masked and kb * sk > qb * sq + sq - 1)]

        def scores(kb, qb, c):
            ks = slice(kb * sk, (kb + 1) * sk)
            s = lax.dot_general(qcs[qb, c], k_ref[ks, :], (((1,), (1,)), ((), ())), preferred_element_type=F32)
            if fox:
                s = s - cum_ref[c:c + 1, ks]
            if masked and kb * sk + sk - 1 > qb * sq:
                row = lax.broadcasted_iota(jnp.int32, (sq, sk), 0) + qb * sq
                col = lax.broadcasted_iota(jnp.int32, (sq, sk), 1) + kb * sk
                s = jnp.where(col <= row, s, NEG_BIG)
            return s

        s_next = scores(*pieces[0])
        for idx, (kb, qb, c) in enumerate(pieces):
            s = s_next
            if idx + 1 < len(pieces):
                s_next = scores(*pieces[idx + 1])
            m_prev, l_prev, acc_prev = state[qb, c]
            m_new = jnp.maximum(m_prev, jnp.broadcast_to(jnp.max(s, axis=1, keepdims=True), (sq, LANES)))
            alpha = jnp.exp(m_prev - m_new)
            pr = jnp.exp(s - jnp.concatenate([m_new] * (sk // LANES), axis=1))
            v_ext = jnp.concatenate([v_ref[kb * sk:(kb + 1) * sk, :], jnp.ones((sk, LANES), BF16)], axis=1)
            r = jnp.dot(pr.astype(BF16), v_ext, preferred_element_type=F32)
            l_new = alpha * l_prev + r[:, LANES:]
            acc_new = alpha * acc_prev + r[:, :LANES]
            state[qb, c] = (m_new, l_new, acc_new)
        for qb in range(nq):
            qs = slice(qb * sq, (qb + 1) * sq)
            for c in range(2):
                m_sc[c, qs, :], l_sc[c, qs, :], acc_sc[c, qs, :] = state[qb, c]

    @pl.when(ki < qi)
    def _():
        step(False)

    @pl.when(ki == qi)
    def _():
        step(True)
        o1 = acc_sc[0] / l_sc[0]
        o2 = acc_sc[1] / l_sc[1]
        if fox:
            o = jnp.where(low, o1, o2)
        else:
            o = _diff_combine(o1, o2, lamv_ref[...], gsub_ref[...], lam_init)
        o_ref[...] = o.astype(o_ref.dtype)


def _flash(q, k, v, extra, *, fox, lam_init):
    b, s, w = q.shape
    g = w // LANES
    t = min(ATT_TILE, s)
    nt = s // t
    pairs = [(i, j) for i in range(nt) for j in range(i + 1)]
    qi_tab = jnp.asarray([pq for pq, _ in pairs], jnp.int32)
    ki_tab = jnp.asarray([pk for _, pk in pairs], jnp.int32)
    q_spec = pl.BlockSpec((None, t, LANES), lambda bb, gg, p, qt, kt: (bb, qt[p], gg))
    kv_spec = pl.BlockSpec((None, t, LANES), lambda bb, gg, p, qt, kt: (bb, kt[p], gg))
    if fox:
        extra_specs = [pl.BlockSpec((None, None, 2, t), lambda bb, gg, p, qt, kt: (bb, gg, 0, kt[p]))]
    else:
        extra_specs = [pl.BlockSpec(extra[0].shape, lambda bb, gg, p, qt, kt: (0, 0)),
                       pl.BlockSpec(extra[1].shape, lambda bb, gg, p, qt, kt: (0, 0))]
    return pl.pallas_call(
        functools.partial(_flash_body, fox=fox, lam_init=lam_init),
        grid_spec=pltpu.PrefetchScalarGridSpec(
            num_scalar_prefetch=2,
            grid=(b, g, len(pairs)),
            in_specs=[q_spec, kv_spec, kv_spec] + extra_specs,
            out_specs=pl.BlockSpec((None, t, LANES), lambda bb, gg, p, qt, kt: (bb, qt[p], gg)),
            scratch_shapes=[pltpu.VMEM((2, t, LANES), F32), pltpu.VMEM((2, t, LANES), F32),
                            pltpu.VMEM((2, t, LANES), F32)]),
        out_shape=jax.ShapeDtypeStruct((b, s, w), BF16),
        compiler_params=_cparams(("parallel", "parallel", "arbitrary")),
        name="flash_fox" if fox else "flash_diff",
    )(qi_tab, ki_tab, q, k, v, *extra)


def _decode_body(pt_ref, *refs, n_pages_step, n_diff_heads, n_fox_heads, n_new, lam_init):
    npg = n_pages_step
    it = iter(refs)
    kd_refs = [next(it) for _ in range(npg)]
    vd_refs = [next(it) for _ in range(npg)]
    kf_refs = [next(it) for _ in range(npg)]
    vf_refs = [next(it) for _ in range(npg)]
    lf_refs = [next(it) for _ in range(npg)]
    (qd_ref, qf_ref, kdn_ref, vdn_ref, kfn_ref, vfn_ref, lfn_ref, lamv_ref, gsub_ref,
     od_ref, of_ref, qd_sc, qf_sc, md_sc, ld_sc, accd_sc, mf_sc, lf_sc, accf_sc, off_sc,
     kd_sc, vd_sc, kf_sc, vf_sc, mask_sc) = it
    j = pl.program_id(1)
    last = pl.num_programs(1) - 1
    hd, hf, t = n_diff_heads, n_fox_heads, n_new
    rows_d = hd * 2 * t
    rows_f = hf * t
    wf = hf * HEAD_DIM

    @pl.when(j == 0)
    def _():
        lane = lax.broadcasted_iota(jnp.int32, (t, LANES), 1)
        qd = qd_ref[...]
        for h in range(hd):
            blk = qd[:, h * LANES:(h + 1) * LANES]
            zero = jnp.zeros_like(blk)
            qd_sc[(2 * h) * t:(2 * h + 1) * t, :] = jnp.where(lane < HEAD_DIM, blk, zero)
            qd_sc[(2 * h + 1) * t:(2 * h + 2) * t, :] = jnp.where(lane < HEAD_DIM, zero, blk)
        qf = qf_ref[...]
        lane_f = lax.broadcasted_iota(jnp.int32, (t, wf), 1)
        for h in range(hf):
            qf_sc[h * t:(h + 1) * t, :] = jnp.where(lane_f // HEAD_DIM == h, qf, jnp.zeros_like(qf))
        md_sc[...] = jnp.full(md_sc.shape, -jnp.inf, F32)
        ld_sc[...] = jnp.zeros(ld_sc.shape, F32)
        accd_sc[...] = jnp.zeros(accd_sc.shape, F32)
        mf_sc[...] = jnp.full(mf_sc.shape, -jnp.inf, F32)
        lf_sc[...] = jnp.zeros(lf_sc.shape, F32)
        accf_sc[...] = jnp.zeros(accf_sc.shape, F32)
        off_sc[...] = jnp.zeros(off_sc.shape, F32)
        r = lax.broadcasted_iota(jnp.int32, mask_sc.shape, 0)
        c = lax.broadcasted_iota(jnp.int32, mask_sc.shape, 1)
        mask_sc[...] = jnp.where((c % hd) == (r // (2 * t)), 0.0, NEG_BIG)

    def online(s, m_sc, l_sc, acc_sc, pv):
        m_prev = m_sc[...]
        m_new = jnp.maximum(m_prev, jnp.max(s, axis=1, keepdims=True))
        alpha = jnp.exp(m_prev - m_new)
        pr = jnp.exp(s - m_new)
        l_sc[...] = alpha * l_sc[...] + jnp.sum(pr, axis=1, keepdims=True)
        acc_sc[...] = alpha * acc_sc[...] + pv(pr.astype(BF16))
        m_sc[...] = m_new

    def diff_page(k_rows, v_rows, causal):
        nr = k_rows.shape[0]
        kb = k_rows.astype(BF16)
        vb = v_rows.astype(BF16)
        s = lax.dot_general(qd_sc[...], kb, (((1,), (1,)), ((), ())), preferred_element_type=F32)
        r = lax.broadcasted_iota(jnp.int32, (rows_d, nr), 0)
        c = lax.broadcasted_iota(jnp.int32, (rows_d, nr), 1)
        ok = (c % hd) == (r // (2 * t))
        if causal:
            ok = ok & ((c // hd) <= (r % t))
        s = jnp.where(ok, s, NEG_BIG)
        online(s, md_sc, ld_sc, accd_sc, lambda pb: jnp.dot(pb, vb, preferred_element_type=F32))

    def fox_page(kt, vt, cum_in, causal):
        nt = kt.shape[1]
        kb = kt.astype(BF16)
        vb = vt.astype(BF16)
        s = jnp.dot(qf_sc[...], kb, preferred_element_type=F32)
        cum = cum_in + off_sc[...]
        bias = jnp.concatenate([jnp.broadcast_to(cum[h:h + 1, :], (t, nt)) for h in range(hf)], axis=0)
        s = s - bias
        if causal:
            r = lax.broadcasted_iota(jnp.int32, (rows_f, nt), 0)
            c = lax.broadcasted_iota(jnp.int32, (rows_f, nt), 1)
            s = jnp.where(c <= (r % t), s, NEG_BIG)
        online(s, mf_sc, lf_sc, accf_sc,
               lambda pb: lax.dot_general(pb, vb, (((1,), (1,)), ((), ())), preferred_element_type=F32))
        off_sc[...] = off_sc[...] + cum_in[:, nt - 1:nt]

    @pl.when(j < last)
    def _():
        rp = kd_refs[0].shape[0]
        pg = kf_refs[0].shape[1]
        for i in range(npg):
            kd_sc[i * rp:(i + 1) * rp, :] = kd_refs[i][...].astype(BF16)
            vd_sc[i * rp:(i + 1) * rp, :] = vd_refs[i][...].astype(BF16)
            kf_sc[:, i * pg:(i + 1) * pg] = kf_refs[i][...].astype(BF16)
            vf_sc[:, i * pg:(i + 1) * pg] = vf_refs[i][...].astype(BF16)
        s = lax.dot_general(qd_sc[...], kd_sc[...], (((1,), (1,)), ((), ())), preferred_element_type=F32)
        s = s + jnp.concatenate([mask_sc[...]] * npg, axis=1)
        online(s, md_sc, ld_sc, accd_sc, lambda pb: jnp.dot(pb, vd_sc[...], preferred_element_type=F32))
        sf = jnp.dot(qf_sc[...], kf_sc[...], preferred_element_type=F32)
        off = off_sc[...]
        biases = []
        for i in range(npg):
            cum_in = lf_refs[i][...]
            cum = cum_in + off
            biases.append(jnp.concatenate([jnp.broadcast_to(cum[h:h + 1, :], (t, pg)) for h in range(hf)], axis=0))
            off = off + cum_in[:, pg - 1:pg]
        off_sc[...] = off
        sf = sf - jnp.concatenate(biases, axis=1)
        online(sf, mf_sc, lf_sc, accf_sc,
               lambda pb: lax.dot_general(pb, vf_sc[...], (((1,), (1,)), ((), ())), preferred_element_type=F32))

    @pl.when(j == last)
    def _():
        diff_page(kdn_ref[...], vdn_ref[...], True)
        fox_page(kfn_ref[...], vfn_ref[...], lfn_ref[...], True)
        od = accd_sc[...] / ld_sc[...]
        for h in range(hd):
            o1 = od[(2 * h) * t:(2 * h + 1) * t, :]
            o2 = od[(2 * h + 1) * t:(2 * h + 2) * t, :]
            o = _diff_combine(o1, o2, lamv_ref[...], gsub_ref[...], lam_init)
            od_ref[:, h * LANES:(h + 1) * LANES] = o.astype(od_ref.dtype)
        of_all = accf_sc[...] / lf_sc[...]
        lane_f = lax.broadcasted_iota(jnp.int32, (t, wf), 1)
        acc = jnp.zeros((t, wf), F32)
        for h in range(hf):
            acc = acc + jnp.where(lane_f // HEAD_DIM == h, of_all[h * t:(h + 1) * t, :], 0.0)
        of_ref[...] = acc.astype(of_ref.dtype)


def _decode(page_table, ckd, cvd, ckf, cvf, clf, qd, qf, kdn, vdn, kfn, vfn, lfn, lamv, gsub, lam_init):
    b, n_pages = page_table.shape
    t, wd = qd.shape[1], qd.shape[2]
    wf = qf.shape[2]
    hd = wd // LANES
    hf = wf // HEAD_DIM
    npg = math.gcd(PAGES_PER_STEP, n_pages)
    steps = n_pages // npg

    def page_spec(arr, i):
        def imap(bb, jj, pt):
            return (pt[bb, jnp.minimum(jj, steps - 1) * npg + i], 0, 0)
        return pl.BlockSpec((None,) + arr.shape[1:], imap)

    per_b = lambda arr: pl.BlockSpec((None,) + arr.shape[1:], lambda bb, jj, pt: (bb, 0, 0))
    full = lambda arr: pl.BlockSpec(arr.shape, lambda bb, jj, pt: (0, 0))
    in_specs = []
    args = []
    for arr in (ckd, cvd, ckf, cvf, clf):
        for i in range(npg):
            in_specs.append(page_spec(arr, i))
            args.append(arr)
    for arr in (qd, qf, kdn, vdn, kfn, vfn, lfn):
        in_specs.append(per_b(arr))
        args.append(arr)
    in_specs += [full(lamv), full(gsub)]
    args += [lamv, gsub]
    rows_d = hd * 2 * t
    rows_f = hf * t
    return pl.pallas_call(
        functools.partial(_decode_body, n_pages_step=npg, n_diff_heads=hd, n_fox_heads=hf, n_new=t,
                          lam_init=lam_init),
        grid_spec=pltpu.PrefetchScalarGridSpec(
            num_scalar_prefetch=1,
            grid=(b, steps + 1),
            in_specs=in_specs,
            out_specs=[pl.BlockSpec((None, t, wd), lambda bb, jj, pt: (bb, 0, 0)),
                       pl.BlockSpec((None, t, wf), lambda bb, jj, pt: (bb, 0, 0))],
            scratch_shapes=[pltpu.VMEM((rows_d, LANES), BF16), pltpu.VMEM((rows_f, wf), BF16),
                            pltpu.VMEM((rows_d, 1), F32), pltpu.VMEM((rows_d, 1), F32),
                            pltpu.VMEM((rows_d, LANES), F32),
                            pltpu.VMEM((rows_f, 1), F32), pltpu.VMEM((rows_f, 1), F32),
                            pltpu.VMEM((rows_f, wf), F32),
                            pltpu.VMEM((hf, 1), F32),
                            pltpu.VMEM((npg * ckd.shape[1], LANES), BF16),
                            pltpu.VMEM((npg * ckd.shape[1], LANES), BF16),
                            pltpu.VMEM((wf, npg * ckf.shape[2]), BF16),
                            pltpu.VMEM((wf, npg * ckf.shape[2]), BF16),
                            pltpu.VMEM((rows_d, ckd.shape[1]), F32)]),
        out_shape=[jax.ShapeDtypeStruct((b, t, wd), BF16), jax.ShapeDtypeStruct((b, t, wf), BF16)],
        compiler_params=_cparams(("parallel", "arbitrary")),
        name="decode",
    )(page_table, *args)


def _outproj_body(od_ref, of_ref, x_ref, ga_ref, sc_ref, sh_ref, g_ref, wo_ref, wr_ref, br_ref,
                  x1_ref, h_ref, idx_ref, gate_ref, *, n_experts):
    tm = x_ref.shape[0]
    wd = od_ref.shape[1]
    o = (jnp.dot(od_ref[...], wo_ref[:wd, :], preferred_element_type=F32)
         + jnp.dot(of_ref[...], wo_ref[wd:, :], preferred_element_type=F32))
    x1 = x_ref[...] + ga_ref[...] * o
    x1_ref[...] = x1
    h = _rms_mod(x1, g_ref[...], sc_ref[...], sh_ref[...])
    _store_row_tiles(h_ref, h)
    logits = jnp.dot(h, wr_ref[...], preferred_element_type=F32, precision=lax.Precision.HIGHEST) + br_ref[...]
    lane = lax.broadcasted_iota(jnp.int32, (tm, LANES), 1).astype(F32)
    cur = jnp.where(lane < n_experts, logits, -jnp.inf)
    idx_out = jnp.zeros((tm, LANES), F32)
    val_out = jnp.zeros((tm, LANES), F32)
    top = None
    den = jnp.zeros((tm, 1), F32)
    for k in range(TOP_K):
        mx = jnp.max(cur, axis=1, keepdims=True)
        sel = jnp.min(jnp.where(cur == mx, lane, float(LANES)), axis=1, keepdims=True)
        if k == 0:
            top = mx
        e = jnp.exp(mx - top)
        den = den + e
        idx_out = jnp.where(lane == k, sel, idx_out)
        val_out = jnp.where(lane == k, e, val_out)
        cur = jnp.where(lane == sel, -jnp.inf, cur)
    idx_ref[...] = idx_out.astype(jnp.int32)
    gate_ref[...] = val_out / den


def _outproj(od, of, x, ga, sc, sh, g2, w_out, w_r, b_r, rows_per_group, n_experts):
    n, d = x.shape
    tm = ROW_TILE
    tiles_per_group = rows_per_group // tm if rows_per_group >= tm else None
    if tiles_per_group is not None:
        mod_spec = pl.BlockSpec((None, 1, d), lambda i: (i // tiles_per_group, 0, 0))
    else:
        mod_spec = pl.BlockSpec((None, tm, d), lambda i: (i, 0, 0))
    row = lambda w: pl.BlockSpec((tm, w), lambda i: (i, 0))
    full = lambda a: pl.BlockSpec(a.shape, lambda i: (0,) * a.ndim)
    return pl.pallas_call(
        functools.partial(_outproj_body, n_experts=n_experts),
        grid=(n // tm,),
        in_specs=[row(od.shape[1]), row(of.shape[1]), row(d), mod_spec, mod_spec, mod_spec,
                  full(g2), full(w_out), full(w_r), full(b_r)],
        out_specs=[row(d), pl.BlockSpec((tm * (d // LANES), LANES), lambda i: (i, 0)), row(LANES), row(LANES)],
        out_shape=[jax.ShapeDtypeStruct((n, d), F32), jax.ShapeDtypeStruct((n * (d // LANES), LANES), F32),
                   jax.ShapeDtypeStruct((n, LANES), jnp.int32), jax.ShapeDtypeStruct((n, LANES), F32)],
        compiler_params=_cparams(("parallel",)),
        name="outproj",
    )(od, of, x, ga, sc, sh, g2, w_out, w_r, b_r)


def _moe_body(be_ref, tok_next_ref, tok_first_ref, dst_prev_ref, dst_last_ref, gate_ref, h_hbm,
              wgu_ref, bgu_ref, wd_ref, bd_ref, y_hbm, tok_sm, dst_sm, xbuf, ybuf, sem_idx, sem_g, sem_s):
    del be_ref
    i = pl.program_id(0)
    n = pl.num_programs(0)
    rows = gate_ref.shape[0]
    d = wgu_ref.shape[0]
    rt = d // LANES
    slot = i % 2
    nslot = 1 - slot

    def load_indices(src_ref, dst_smem):
        cp = pltpu.make_async_copy(src_ref.at[0], dst_smem, sem_idx)
        cp.start()
        cp.wait()

    def gather_rows(buf_slot):
        for r in range(rows):
            src = pl.multiple_of(tok_sm[0, r] * rt, rt)
            pltpu.make_async_copy(h_hbm.at[pl.ds(src, rt), :],
                                  xbuf.at[buf_slot, pl.ds(r * rt, rt), :], sem_g.at[buf_slot]).start()

    def scatter_row(buf_slot, r):
        dst = pl.multiple_of(dst_sm[0, r] * rt, rt)
        pltpu.make_async_copy(ybuf.at[buf_slot, pl.ds(r * rt, rt), :],
                              y_hbm.at[pl.ds(dst, rt), :], sem_s.at[buf_slot]).start()

    def scatter_rows(buf_slot):
        for r in range(rows):
            scatter_row(buf_slot, r)

    def wait_gather(buf_slot):
        pltpu.make_async_copy(h_hbm.at[pl.ds(0, rows * rt), :], xbuf.at[buf_slot], sem_g.at[buf_slot]).wait()

    def wait_scatter(buf_slot):
        pltpu.make_async_copy(ybuf.at[buf_slot], y_hbm.at[pl.ds(0, rows * rt), :], sem_s.at[buf_slot]).wait()

    @pl.when(i == 0)
    def _():
        load_indices(tok_first_ref, tok_sm)
        gather_rows(0)
        ybuf[1] = jnp.zeros(ybuf.shape[1:], F32)

    @pl.when(i >= 1)
    def _():
        wait_scatter(slot)

    wait_gather(slot)
    load_indices(tok_next_ref, tok_sm)
    load_indices(dst_prev_ref, dst_sm)
    gather_rows(nslot)
    scatter_rows(nslot)

    x = _load_row_tiles(xbuf, rows, d, lead=(slot,)).astype(BF16)
    gu = jnp.dot(x, wgu_ref[...], preferred_element_type=F32) + bgu_ref[...]
    dff = gu.shape[1] // 2
    g = jnp.minimum(gu[:, :dff], SWIGLU_LIMIT)
    u = jnp.clip(gu[:, dff:], -SWIGLU_LIMIT, SWIGLU_LIMIT)
    act = (u + 1.0) * g * jax.nn.sigmoid(SWIGLU_ALPHA * g)
    y = jnp.dot(act.astype(BF16), wd_ref[...], preferred_element_type=F32) + bd_ref[...]
    _store_row_tiles(ybuf, y * gate_ref[...], lead=(slot,))

    @pl.when(i == n - 1)
    def _():
        load_indices(dst_last_ref, dst_sm)

        def scatter(r, carry):
            scatter_row(slot, r)
            return carry
        lax.fori_loop(0, rows, scatter, 0, unroll=8)
        wait_gather(nslot)
        wait_scatter(nslot)
        wait_scatter(slot)


def _moe(h, idx, gates, w_gu, b_gu, w_down, b_down):
    d = w_gu.shape[1]
    rt = d // LANES
    n = h.shape[0] // rt
    n_experts = w_gu.shape[0]
    blk = EXPERT_ROWS
    n_asg = n * TOP_K
    n_blocks = -(-n_asg // blk) + n_experts
    n_slots = n_blocks * blk
    n_pad = n_slots - n_asg
    flat_e = idx.reshape(-1)
    order = jnp.argsort(flat_e).astype(jnp.int32)
    e_sorted = flat_e[order]
    start = jnp.searchsorted(e_sorted, jnp.arange(n_experts + 1, dtype=jnp.int32), side='left').astype(jnp.int32)
    counts = start[1:] - start[:-1]
    start = start[:-1]
    padded = (counts + blk - 1) // blk * blk
    pad_end = jnp.cumsum(padded)
    pad_start = pad_end - padded
    block_expert = jnp.minimum(
        jnp.searchsorted(pad_end, jnp.arange(n_blocks, dtype=jnp.int32) * blk, side='right'),
        n_experts - 1).astype(jnp.int32)
    slot_e = jnp.repeat(block_expert, blk)
    pos = jnp.arange(n_slots, dtype=jnp.int32) - pad_start[slot_e]
    valid = pos < counts[slot_e]
    src = order[jnp.clip(start[slot_e] + pos, 0, n_asg - 1)]
    pad_row = n_asg + (pad_start[slot_e] - start[slot_e]) + (pos - counts[slot_e])
    slot_tok = jnp.where(valid, src // TOP_K, 0).astype(jnp.int32)
    slot_dst = jnp.where(valid, (src % TOP_K) * n + src // TOP_K, pad_row).astype(jnp.int32)
    slot_gate = jnp.where(valid, gates.reshape(-1)[src], 0.0)
    dump_block = n_asg + n_pad + jnp.arange(blk, dtype=jnp.int32)
    slot_tok = slot_tok.reshape(n_blocks, 1, blk)
    slot_dst = jnp.concatenate([dump_block, slot_dst]).reshape(n_blocks + 1, 1, blk)

    y = pl.pallas_call(
        _moe_body,
        grid_spec=pltpu.PrefetchScalarGridSpec(
            num_scalar_prefetch=1,
            grid=(n_blocks,),
            in_specs=[
                pl.BlockSpec((1, 1, blk), lambda i, be: (jnp.minimum(i + 1, n_blocks - 1), 0, 0)),
                pl.BlockSpec((1, 1, blk), lambda i, be: (0, 0, 0)),
                pl.BlockSpec((1, 1, blk), lambda i, be: (i, 0, 0)),
                pl.BlockSpec((1, 1, blk), lambda i, be: (n_blocks, 0, 0)),
                pl.BlockSpec((blk, 1), lambda i, be: (i, 0)),
                pl.BlockSpec(memory_space=pl.ANY),
                pl.BlockSpec((None, d, w_gu.shape[2]), lambda i, be: (be[i], 0, 0)),
                pl.BlockSpec((None, 1, b_gu.shape[2]), lambda i, be: (be[i], 0, 0)),
                pl.BlockSpec((None, w_down.shape[1], d), lambda i, be: (be[i], 0, 0)),
                pl.BlockSpec((None, 1, d), lambda i, be: (be[i], 0, 0)),
            ],
            out_specs=pl.BlockSpec(memory_space=pl.ANY),
            scratch_shapes=[pltpu.SMEM((1, blk), jnp.int32), pltpu.SMEM((1, blk), jnp.int32),
                            pltpu.VMEM((2, blk * rt, LANES), F32), pltpu.VMEM((2, blk * rt, LANES), F32),
                            pltpu.SemaphoreType.DMA, pltpu.SemaphoreType.DMA((2,)),
                            pltpu.SemaphoreType.DMA((2,))]),
        out_shape=jax.ShapeDtypeStruct(((n_asg + n_pad + blk) * rt, LANES), F32),
        compiler_params=_cparams(("arbitrary",)),
        name="moe",
    )(block_expert, slot_tok, slot_tok, slot_dst, slot_dst, slot_gate.reshape(n_slots, 1),
      h, w_gu, b_gu, w_down, b_down)
    return y


def _final_body(x_ref, ga_ref, y0_ref, y1_ref, y2_ref, y3_ref, g_ref, o_ref):
    tm, d = x_ref.shape
    y0, y1, y2, y3 = (_load_row_tiles(r, tm, d) for r in (y0_ref, y1_ref, y2_ref, y3_ref))
    moe = (y0 + y1) + (y2 + y3)
    x = x_ref[...] + ga_ref[...] * moe
    ms = jnp.mean(x * x, axis=-1, keepdims=True)
    o_ref[...] = x * lax.rsqrt(ms + NORM_EPS) * g_ref[...]


def _final(x1, ga, y_un, g_final, rows_per_group):
    n, d = x1.shape
    tm = ROW_TILE
    nt = n // tm
    tiles_per_group = rows_per_group // tm if rows_per_group >= tm else None
    if tiles_per_group is not None:
        mod_spec = pl.BlockSpec((None, 1, d), lambda i: (i // tiles_per_group, 0, 0))
    else:
        mod_spec = pl.BlockSpec((None, tm, d), lambda i: (i, 0, 0))
    yk = lambda k: pl.BlockSpec((tm * (d // LANES), LANES), lambda i: (k * nt + i, 0))
    return pl.pallas_call(
        _final_body,
        grid=(nt,),
        in_specs=[pl.BlockSpec((tm, d), lambda i: (i, 0)), mod_spec,
                  yk(0), yk(1), yk(2), yk(3), pl.BlockSpec((1, d), lambda i: (0, 0))],
        out_specs=pl.BlockSpec((tm, d), lambda i: (i, 0)),
        out_shape=jax.ShapeDtypeStruct((n, d), F32),
        compiler_params=_cparams(("parallel",)),
        name="final",
    )(x1, ga, y_un, y_un, y_un, y_un, g_final)


def _rope_tables(pos):
    half = HEAD_DIM // 2
    inv = 1.0 / (ROPE_THETA ** (jnp.arange(half, dtype=F32) / half))
    ang = pos.astype(F32)[:, None] * inv[None, :]
    cos = jnp.tile(jnp.cos(ang), (1, LANES // half))
    sin = jnp.sin(ang)
    sin = jnp.tile(jnp.concatenate([-sin, sin], axis=1), (1, LANES // HEAD_DIM))
    return cos, sin


def _mods_for(m, rows_per_group):
    g = m.shape[0]
    parts = jnp.split(m, 6, axis=-1)
    if rows_per_group >= ROW_TILE:
        return [p[:, None, :] for p in parts]
    groups_per_tile = ROW_TILE // rows_per_group
    return [jnp.repeat(p, rows_per_group, axis=0).reshape(g // groups_per_tile, ROW_TILE, -1) for p in parts]


def kernel(x_prompt, x_sample, c_prompt, c_sample, cache_k_diff, cache_v_diff, cache_k_fox, cache_v_fox, cache_logf_fox, page_table, w_ada, b_ada, g_norm1, g_norm2, w_in, b_fgate, lam_q1, lam_k1, lam_q2, lam_k2, g_subln, w_out, w_router, b_router, w_gu, b_gu, w_down, b_down, g_final):
    bp, s, d = x_prompt.shape
    bs, t, _ = x_sample.shape
    depth = w_ada.shape[0]
    assert depth == 1
    l = 0
    lam_init = 0.8 - 0.6 * math.exp(-0.3 * l)
    n_pool, page = cache_k_diff.shape[1], cache_k_diff.shape[2]
    hd = cache_k_diff.shape[3]
    hf = cache_k_fox.shape[3]
    wd = hd * 2 * HEAD_DIM
    wf = hf * HEAD_DIM
    n_experts = w_router.shape[2]
    past = page_table.shape[1] * page

    scale = HEAD_DIM ** -0.5
    w_in_l = w_in[l]
    col_scale = jnp.concatenate([jnp.full((wd,), scale, F32), jnp.ones((2 * wd,), F32),
                                 jnp.full((wf,), scale, F32), jnp.ones((2 * wf,), F32)])
    w_main = (w_in_l[:, :3 * wd + 3 * wf] * col_scale[None, :]).astype(BF16)
    w_f = jnp.pad(w_in_l[:, 3 * wd + 3 * wf:], ((0, 0), (0, LANES - hf))).astype(BF16)
    b_f = jnp.pad(b_fgate[l], (0, LANES - hf)).reshape(1, LANES)
    w_out_b = w_out[l].astype(BF16)
    w_r = jnp.pad(w_router[l], ((0, 0), (0, LANES - n_experts)))
    b_r = jnp.pad(b_router[l], (0, LANES - n_experts)).reshape(1, LANES)
    w_gu_b = w_gu[l].astype(BF16)
    w_down_b = w_down[l].astype(BF16)
    b_gu_l = b_gu[l][:, None, :]
    b_down_l = b_down[l][:, None, :]
    g1 = g_norm1[l].reshape(1, d)
    g2 = g_norm2[l].reshape(1, d)
    gf = g_final.reshape(1, d)
    lamv = jnp.stack([lam_q1[l], lam_k1[l], lam_q2[l], lam_k2[l]], axis=0)
    gsub = g_subln[l].reshape(1, 2 * HEAD_DIM)

    mods = _adaln(jnp.concatenate([c_prompt, c_sample], axis=0), w_ada[l], b_ada[l])
    sh1p, sc1p, ga1p, sh2p, sc2p, ga2p = _mods_for(mods[:bp], s)
    sh1s, sc1s, ga1s, sh2s, sc2s, ga2s = _mods_for(mods[bp:], t)

    cos_p, sin_p = _rope_tables(jnp.arange(s))
    xp = x_prompt.reshape(bp * s, d)
    (qd, kd32, kd16, vd32, vd16, qf, kf32, kf16, vf32, vf16, lfp) = _inproj(
        xp, sc1p, sh1p, g1, cos_p, sin_p, w_main, w_f, b_f, s, s // ROW_TILE)
    logf_p = lfp[:, :hf]
    chunks = s // LANES
    lf_rows = jnp.transpose(logf_p.reshape(bp, s, hf), (0, 2, 1)).reshape(bp * hf * chunks, LANES)
    cum = _cumsum_rows(lf_rows, hf * chunks, chunks).reshape(bp, hf // 2, 2, s)
    b3 = lambda a: a.reshape(bp, s, -1)
    od_p = _flash(b3(qd), b3(kd16), b3(vd16), (lamv, gsub), fox=False, lam_init=lam_init)
    of_p = _flash(b3(qf), b3(kf16), b3(vf16), (cum,), fox=True, lam_init=lam_init)
    x1p, h2p, idxp, gatep = _outproj(od_p.reshape(bp * s, wd), of_p.reshape(bp * s, wf), xp,
                                     ga1p, sc2p, sh2p, g2, w_out_b, w_r, b_r, s, n_experts)
    yp = _moe(h2p, idxp[:, :TOP_K], gatep[:, :TOP_K], w_gu_b, b_gu_l, w_down_b, b_down_l)
    y_prompt = _final(x1p, ga2p, yp, gf, s).reshape(bp, s, d)

    cos_s, sin_s = _rope_tables(past + jnp.arange(t))
    reps = ROW_TILE // t
    cos_s = jnp.tile(cos_s, (reps, 1))
    sin_s = jnp.tile(sin_s, (reps, 1))
    xs = x_sample.reshape(bs * t, d)
    (qds, kd32s, _u0, vd32s, _u1, qfs, kf32s, _u2, vf32s, _u3, lfs) = _inproj(
        xs, sc1s, sh1s, g1, cos_s, sin_s, w_main, w_f, b_f, t, 1)
    logf_s = lfs[:, :hf]
    ckd = cache_k_diff[l].reshape(n_pool, page * hd, 2 * HEAD_DIM)
    cvd = cache_v_diff[l].reshape(n_pool, page * hd, 2 * HEAD_DIM)
    ckf = jnp.transpose(cache_k_fox[l], (0, 2, 3, 1)).reshape(n_pool, wf, page)
    cvf = jnp.transpose(cache_v_fox[l], (0, 2, 3, 1)).reshape(n_pool, wf, page)
    clf = jnp.transpose(cache_logf_fox[l], (0, 2, 1)).reshape(n_pool * hf, page)
    clf_cum = _cumsum_rows(clf, math.gcd(2048, n_pool * hf), 1).reshape(n_pool, hf, page)
    tp = 16
    padt = lambda a: jnp.pad(a.reshape(bs, t, -1), ((0, 0), (0, tp - t), (0, 0)))
    kdn = padt(kd32s).reshape(bs, tp * hd, 2 * HEAD_DIM)
    vdn = padt(vd32s).reshape(bs, tp * hd, 2 * HEAD_DIM)
    padl = lambda a: jnp.pad(jnp.transpose(a.reshape(bs, t, -1), (0, 2, 1)), ((0, 0), (0, 0), (0, LANES - t)))
    kfn = padl(kf32s)
    vfn = padl(vf32s)
    lfn = _cumsum_rows(padl(logf_s).reshape(bs * hf, LANES), bs * hf, 1).reshape(bs, hf, LANES)
    od_s, of_s = _decode(page_table, ckd, cvd, ckf, cvf, clf_cum, qds.reshape(bs, t, wd), qfs.reshape(bs, t, wf),
                         kdn, vdn, kfn, vfn, lfn, lamv, gsub, lam_init)
    x1s, h2s, idxs, gates = _outproj(od_s.reshape(bs * t, wd), of_s.reshape(bs * t, wf), xs,
                                     ga1s, sc2s, sh2s, g2, w_out_b, w_r, b_r, t, n_experts)
    ys = _moe(h2s, idxs[:, :TOP_K], gates[:, :TOP_K], w_gu_b, b_gu_l, w_down_b, b_down_l)
    y_sample = _final(x1s, ga2s, ys, gf, t).reshape(bs, t, d)

    rp = lambda a, h_, c_: a.reshape(1, bp, s, h_, c_)
    rs = lambda a, h_, c_: a.reshape(1, bs, t, h_, c_)
    rpt = lambda a: jnp.transpose(a.reshape(bp, hf, HEAD_DIM, s), (0, 3, 1, 2))[None]
    return (y_prompt, y_sample,
            rp(kd32, hd, 2 * HEAD_DIM), rp(vd32, hd, 2 * HEAD_DIM), rpt(kf32), rpt(vf32),
            logf_p.reshape(1, bp, s, hf),
            rs(kd32s, hd, 2 * HEAD_DIM), rs(vd32s, hd, 2 * HEAD_DIM), rs(kf32s, hf, HEAD_DIM), rs(vf32s, hf, HEAD_DIM),
            logf_s.reshape(1, bs, t, hf))
```

```python
import functools
import math

import jax
import jax.numpy as jnp
import numpy as np
from jax import lax
from jax.experimental import pallas as pl
from jax.experimental.pallas import tpu as pltpu

HEAD_DIM = 64
LANES = 128
ROPE_THETA = 10000.0
NORM_EPS = 1e-5
TOP_K = 4
SWIGLU_ALPHA = 1.702
SWIGLU_LIMIT = 7.0
NEG_BIG = -1e30
ROW_TILE = 512
ATT_TILE = 1024
ATT_SUB_Q = 256
ATT_SUB_K = 512
DMA_QUEUES = 2
EXPERT_ROWS = 256
PAGES_PER_STEP = 8
VMEM_LIMIT = 56 * 1024 * 1024

F32 = jnp.float32
BF16 = jnp.bfloat16


def _cparams(sem):
    return pltpu.CompilerParams(dimension_semantics=sem, vmem_limit_bytes=VMEM_LIMIT)


def _store_row_tiles(ref, val, lead=()):
    rows, d = val.shape
    k = d // LANES
    for j in range(k):
        ref[lead + (pl.ds(j, rows, stride=k), slice(None))] = val[:, j * LANES:(j + 1) * LANES]


def _load_row_tiles(ref, rows, d, lead=()):
    k = d // LANES
    return jnp.concatenate([ref[lead + (pl.ds(j, rows, stride=k), slice(None))] for j in range(k)], axis=1)


def _adaln_body(c_ref, w_ref, b_ref, o_ref):
    c = c_ref[...]
    s = (c * jax.nn.sigmoid(c)).astype(BF16)
    o_ref[...] = jnp.dot(s, w_ref[...].astype(BF16), preferred_element_type=F32) + b_ref[...]


def _adaln(c, w, b):
    n, d = c.shape
    nout = w.shape[1]
    tn = 1024
    return pl.pallas_call(
        _adaln_body,
        grid=(nout // tn,),
        in_specs=[pl.BlockSpec((n, d), lambda j: (0, 0)),
                  pl.BlockSpec((d, tn), lambda j: (0, j)),
                  pl.BlockSpec((1, tn), lambda j: (0, j))],
        out_specs=pl.BlockSpec((n, tn), lambda j: (0, j)),
        out_shape=jax.ShapeDtypeStruct((n, nout), F32),
        compiler_params=_cparams(("arbitrary",)),
        name="adaln",
    )(c, w, b.reshape(1, nout))


def _rms_mod(x, g, sc, sh):
    ms = jnp.mean(x * x, axis=-1, keepdims=True)
    return (x * lax.rsqrt(ms + NORM_EPS) * g) * (1.0 + sc) + sh


def _inproj_body(x_ref, sc_ref, sh_ref, g_ref, cos_ref, sin_ref, w_ref, wf_ref, bf_ref,
                 qd_ref, kd32_ref, kd16_ref, vd32_ref, vd16_ref,
                 qf_ref, kf32_ref, kf16_ref, vf32_ref, vf16_ref, lf_ref, *, fox_transposed):
    tm = x_ref.shape[0]
    hb = _rms_mod(x_ref[...], g_ref[...], sc_ref[...], sh_ref[...]).astype(BF16)
    cos = cos_ref[...]
    sin = sin_ref[...]
    lane = lax.broadcasted_iota(jnp.int32, (tm, LANES), 1)
    first_half = (lane % HEAD_DIM) < (HEAD_DIM // 2)

    def rope(blk):
        partner = jnp.where(first_half, pltpu.roll(blk, LANES - HEAD_DIM // 2, 1),
                            pltpu.roll(blk, HEAD_DIM // 2, 1))
        return blk * cos + partner * sin

    width = qd_ref.shape[1]
    nblk = width // LANES
    outs = ((qd_ref,), (kd32_ref, kd16_ref), (vd32_ref, vd16_ref),
            (qf_ref,), (kf32_ref, kf16_ref), (vf32_ref, vf16_ref))
    for gidx, refs in enumerate(outs):
        z = jnp.dot(hb, w_ref[:, gidx * width:(gidx + 1) * width], preferred_element_type=F32)
        if fox_transposed and gidx >= 4:
            refs[0][...] = z.T
            refs = refs[1:]
        for j in range(nblk):
            blk = z[:, j * LANES:(j + 1) * LANES]
            if gidx < 2:
                blk = rope(blk)
            for r in refs:
                if r is kd32_ref or r is vd32_ref:
                    r[pl.ds(j, tm, stride=nblk), :] = blk
                else:
                    r[:, j * LANES:(j + 1) * LANES] = blk.astype(r.dtype)
    fl = jnp.dot(hb, wf_ref[...], preferred_element_type=F32) + bf_ref[...]
    lf_ref[...] = jax.nn.log_sigmoid(fl)


def _inproj(x, sc, sh, g1, cos, sin, w_main, w_f, b_f, rows_per_group, pos_tiles):
    n, d = x.shape
    tm = ROW_TILE
    width = w_main.shape[1] // 6
    tiles_per_group = rows_per_group // tm if rows_per_group >= tm else None
    if tiles_per_group is not None:
        mod_spec = pl.BlockSpec((None, 1, d), lambda i: (i // tiles_per_group, 0, 0))
    else:
        mod_spec = pl.BlockSpec((None, tm, d), lambda i: (i, 0, 0))
    tab_spec = pl.BlockSpec((tm, LANES), lambda i: (i % pos_tiles, 0))
    row = lambda w: pl.BlockSpec((tm, w), lambda i: (i, 0))
    full = lambda a: pl.BlockSpec(a.shape, lambda i: (0,) * a.ndim)
    o16 = jax.ShapeDtypeStruct((n, width), BF16)
    nblk = width // LANES
    ort = jax.ShapeDtypeStruct((n * nblk, LANES), F32)
    rt_spec = pl.BlockSpec((tm * nblk, LANES), lambda i: (i, 0))
    fox_transposed = tiles_per_group is not None
    if fox_transposed:
        ofx = jax.ShapeDtypeStruct((n // rows_per_group, width, rows_per_group), F32)
        fx_spec = pl.BlockSpec((None, width, tm), lambda i: (i // tiles_per_group, 0, i % tiles_per_group))
    else:
        ofx = jax.ShapeDtypeStruct((n, width), F32)
        fx_spec = row(width)
    return pl.pallas_call(
        functools.partial(_inproj_body, fox_transposed=fox_transposed),
        grid=(n // tm,),
        in_specs=[row(d), mod_spec, mod_spec, full(g1), tab_spec, tab_spec,
                  full(w_main), full(w_f), full(b_f)],
        out_specs=[row(width), rt_spec, row(width), rt_spec, row(width),
                   row(width), fx_spec, row(width), fx_spec, row(width), row(LANES)],
        out_shape=[o16, ort, o16, ort, o16, o16, ofx, o16, ofx, o16,
                   jax.ShapeDtypeStruct((n, LANES), F32)],
        compiler_params=_cparams(("parallel",)),
        name="inproj",
    )(x, sc, sh, g1, cos, sin, w_main, w_f, b_f)


def _cumsum_body(x_ref, o_ref, *, chunks_per_row):
    r = x_ref.shape[0]
    i0 = lax.broadcasted_iota(jnp.int32, (LANES, LANES), 0)
    i1 = lax.broadcasted_iota(jnp.int32, (LANES, LANES), 1)
    upper = (i0 <= i1).astype(F32)
    cum = jnp.dot(x_ref[...], upper, preferred_element_type=F32, precision=lax.Precision.HIGHEST)
    if chunks_per_row > 1:
        r0 = lax.broadcasted_iota(jnp.int32, (r, r), 0)
        r1 = lax.broadcasted_iota(jnp.int32, (r, r), 1)
        prev = ((r0 // chunks_per_row == r1 // chunks_per_row) & (r1 < r0)).astype(F32)
        tot = jnp.broadcast_to(cum[:, LANES - 1:LANES], (r, LANES))
        cum = cum + jnp.dot(prev, tot, preferred_element_type=F32, precision=lax.Precision.HIGHEST)
    o_ref[...] = cum


def _cumsum_rows(x, rows_per_block, chunks_per_row):
    n = x.shape[0]
    return pl.pallas_call(
        functools.partial(_cumsum_body, chunks_per_row=chunks_per_row),
        grid=(n // rows_per_block,),
        in_specs=[pl.BlockSpec((rows_per_block, LANES), lambda i: (i, 0))],
        out_specs=pl.BlockSpec((rows_per_block, LANES), lambda i: (i, 0)),
        out_shape=jax.ShapeDtypeStruct((n, LANES), F32),
        compiler_params=_cparams(("parallel",)),
        name="cumsum",
    )(x)


def _lambda_from(lamv, lam_init):
    a = jnp.sum(lamv[0:1, :] * lamv[1:2, :], axis=1, keepdims=True)
    b = jnp.sum(lamv[2:3, :] * lamv[3:4, :], axis=1, keepdims=True)
    return jnp.exp(a) - jnp.exp(b) + lam_init


def _diff_combine(o1, o2, lamv, gsub, lam_init):
    o = o1 - _lambda_from(lamv, lam_init) * o2
    ms = jnp.mean(o * o, axis=-1, keepdims=True)
    return (o * lax.rsqrt(ms + NORM_EPS) * gsub) * (1.0 - lam_init)


def _flash_body(qi_tab, ki_tab, *refs, fox, lam_init):
    if fox:
        q_ref, k_ref, v_ref, cum_ref, o_ref, m_sc, l_sc, acc_sc = refs
    else:
        q_ref, k_ref, v_ref, lamv_ref, gsub_ref, o_ref, m_sc, l_sc, acc_sc = refs
    p = pl.program_id(2)
    qi = qi_tab[p]
    ki = ki_tab[p]
    tq = q_ref.shape[0]
    tk = k_ref.shape[0]

    @pl.when(ki == 0)
    def _():
        m_sc[...] = jnp.full(m_sc.shape, -jnp.inf, F32)
        l_sc[...] = jnp.zeros(l_sc.shape, F32)
        acc_sc[...] = jnp.zeros(acc_sc.shape, F32)

    lane = lax.broadcasted_iota(jnp.int32, (tq, LANES), 1)
    low = lane < HEAD_DIM

    sq = min(ATT_SUB_Q, tq)
    sk = min(ATT_SUB_K, tk)

    def step(masked):
        nq = tq // sq
        lo = lax.broadcasted_iota(jnp.int32, (sq, LANES), 1) < HEAD_DIM
        state = {}
        qcs = {}
        for qb in range(nq):
            qs = slice(qb * sq, (qb + 1) * sq)
            q = q_ref[qs, :]
            zero = jnp.zeros_like(q)
            for c in range(2):
                qcs[qb, c] = jnp.where(lo, q, zero) if c == 0 else jnp.where(lo, zero, q)
                state[qb, c] = (m_sc[c, qs, :], l_sc[c, qs, :], acc_sc[c, qs, :])
        pieces = [(kb, qb, c) for kb in range(tk // sk) for qb in range(nq) for c in range(2)
                  if not (masked and kb * sk > qb * sq + sq - 1)]

        def scores(kb, qb, c):
            ks = slice(kb * sk, (kb + 1) * sk)
            s = lax.dot_general(qcs[qb, c], k_ref[ks, :], (((1,), (1,)), ((), ())), preferred_element_type=F32)
            if fox:
                s = s - cum_ref[c:c + 1, ks]
            if masked and kb * sk + sk - 1 > qb * sq:
                row = lax.broadcasted_iota(jnp.int32, (sq, sk), 0) + qb * sq
                col = lax.broadcasted_iota(jnp.int32, (sq, sk), 1) + kb * sk
                s = jnp.where(col <= row, s, NEG_BIG)
            return s

        s_next = scores(*pieces[0])
        for idx, (kb, qb, c) in enumerate(pieces):
            s = s_next
            if idx + 1 < len(pieces):
                s_next = scores(*pieces[idx + 1])
            m_prev, l_prev, acc_prev = state[qb, c]
            m_new = jnp.maximum(m_prev, jnp.broadcast_to(jnp.max(s, axis=1, keepdims=True), (sq, LANES)))
            alpha = jnp.exp(m_prev - m_new)
            pr = jnp.exp(s - jnp.concatenate([m_new] * (sk // LANES), axis=1))
            v_ext = jnp.concatenate([v_ref[kb * sk:(kb + 1) * sk, :], jnp.ones((sk, LANES), BF16)], axis=1)
            r = jnp.dot(pr.astype(BF16), v_ext, preferred_element_type=F32)
            l_new = alpha * l_prev + r[:, LANES:]
            acc_new = alpha * acc_prev + r[:, :LANES]
            state[qb, c] = (m_new, l_new, acc_new)
        for qb in range(nq):
            qs = slice(qb * sq, (qb + 1) * sq)
            for c in range(2):
                m_sc[c, qs, :], l_sc[c, qs, :], acc_sc[c, qs, :] = state[qb, c]

    @pl.when(ki < qi)
    def _():
        step(False)

    @pl.when(ki == qi)
    def _():
        step(True)
        o1 = acc_sc[0] / l_sc[0]
        o2 = acc_sc[1] / l_sc[1]
        if fox:
            o = jnp.where(low, o1, o2)
        else:
            o = _diff_combine(o1, o2, lamv_ref[...], gsub_ref[...], lam_init)
        o_ref[...] = o.astype(o_ref.dtype)


def _flash(q, k, v, extra, *, fox, lam_init):
    b, s, w = q.shape
    g = w // LANES
    t = min(ATT_TILE, s)
    nt = s // t
    pairs = [(i, j) for i in range(nt) for j in range(i + 1)]
    qi_tab = jnp.asarray([pq for pq, _ in pairs], jnp.int32)
    ki_tab = jnp.asarray([pk for _, pk in pairs], jnp.int32)
    q_spec = pl.BlockSpec((None, t, LANES), lambda bb, gg, p, qt, kt: (bb, qt[p], gg))
    kv_spec = pl.BlockSpec((None, t, LANES), lambda bb, gg, p, qt, kt: (bb, kt[p], gg))
    if fox:
        extra_specs = [pl.BlockSpec((None, None, 2, t), lambda bb, gg, p, qt, kt: (bb, gg, 0, kt[p]))]
    else:
        extra_specs = [pl.BlockSpec(extra[0].shape, lambda bb, gg, p, qt, kt: (0, 0)),
                       pl.BlockSpec(extra[1].shape, lambda bb, gg, p, qt, kt: (0, 0))]
    return pl.pallas_call(
        functools.partial(_flash_body, fox=fox, lam_init=lam_init),
        grid_spec=pltpu.PrefetchScalarGridSpec(
            num_scalar_prefetch=2,
            grid=(b, g, len(pairs)),
            in_specs=[q_spec, kv_spec, kv_spec] + extra_specs,
            out_specs=pl.BlockSpec((None, t, LANES), lambda bb, gg, p, qt, kt: (bb, qt[p], gg)),
            scratch_shapes=[pltpu.VMEM((2, t, LANES), F32), pltpu.VMEM((2, t, LANES), F32),
                            pltpu.VMEM((2, t, LANES), F32)]),
        out_shape=jax.ShapeDtypeStruct((b, s, w), BF16),
        compiler_params=_cparams(("parallel", "parallel", "arbitrary")),
        name="flash_fox" if fox else "flash_diff",
    )(qi_tab, ki_tab, q, k, v, *extra)


def _decode_body(pt_ref, *refs, n_pages_step, n_diff_heads, n_fox_heads, n_new, lam_init):
    npg = n_pages_step
    (ckd_hbm, cvd_hbm, ckf_hbm, cvf_hbm, clf_hbm,
     qd_ref, qf_ref, kdn_ref, vdn_ref, kfn_ref, vfn_ref, lfn_ref, lamv_ref, gsub_ref,
     od_ref, of_ref, qd_sc, qf_sc, md_sc, ld_sc, accd_sc, mf_sc, lf_sc, accf_sc, off_sc,
     kd_sc, vd_sc, kf_sc, vf_sc, mask_sc, kd_raw, vd_raw, kf_raw, vf_raw, lf_raw, sem_pg) = refs
    b = pl.program_id(0)
    nb = pl.num_programs(0)
    j = pl.program_id(1)
    last = pl.num_programs(1) - 1
    slot = (b * last + j) % 2

    def page_copies(bq, jq, sl):
        out = []
        for i in range(npg):
            page = pt_ref[bq, jq * npg + i]
            for hbm, raw in ((ckd_hbm, kd_raw), (cvd_hbm, vd_raw), (ckf_hbm, kf_raw), (cvf_hbm, vf_raw),
                             (clf_hbm, lf_raw)):
                out.append(pltpu.make_async_copy(hbm.at[page], raw.at[sl, i], sem_pg.at[sl]))
        return out

    def fetch(bq, jq, sl):
        for cp in page_copies(bq, jq, sl):
            cp.start()

    @pl.when((b == 0) & (j == 0))
    def _():
        fetch(0, 0, 0)

    @pl.when(j + 1 < last)
    def _():
        fetch(b, j + 1, 1 - slot)

    @pl.when((j + 1 == last) & (b + 1 < nb))
    def _():
        fetch(b + 1, 0, 1 - slot)

    kd_refs = [kd_raw.at[slot, i] for i in range(npg)]
    vd_refs = [vd_raw.at[slot, i] for i in range(npg)]
    kf_refs = [kf_raw.at[slot, i] for i in range(npg)]
    vf_refs = [vf_raw.at[slot, i] for i in range(npg)]
    lf_refs = [lf_raw.at[slot, i] for i in range(npg)]
    hd, hf, t = n_diff_heads, n_fox_heads, n_new
    rows_d = hd * 2 * t
    rows_f = hf * t
    wf = hf * HEAD_DIM

    @pl.when(j == 0)
    def _():
        lane = lax.broadcasted_iota(jnp.int32, (t, LANES), 1)
        qd = qd_ref[...]
        for h in range(hd):
            blk = qd[:, h * LANES:(h + 1) * LANES]
            zero = jnp.zeros_like(blk)
            qd_sc[(2 * h) * t:(2 * h + 1) * t, :] = jnp.where(lane < HEAD_DIM, blk, zero)
            qd_sc[(2 * h + 1) * t:(2 * h + 2) * t, :] = jnp.where(lane < HEAD_DIM, zero, blk)
        qf = qf_ref[...]
        lane_f = lax.broadcasted_iota(jnp.int32, (t, wf), 1)
        for h in range(hf):
            qf_sc[h * t:(h + 1) * t, :] = jnp.where(lane_f // HEAD_DIM == h, qf, jnp.zeros_like(qf))
        md_sc[...] = jnp.full(md_sc.shape, -jnp.inf, F32)
        ld_sc[...] = jnp.zeros(ld_sc.shape, F32)
        accd_sc[...] = jnp.zeros(accd_sc.shape, F32)
        mf_sc[...] = jnp.full(mf_sc.shape, -jnp.inf, F32)
        lf_sc[...] = jnp.zeros(lf_sc.shape, F32)
        accf_sc[...] = jnp.zeros(accf_sc.shape, F32)
        off_sc[...] = jnp.zeros(off_sc.shape, F32)
        r = lax.broadcasted_iota(jnp.int32, mask_sc.shape, 0)
        c = lax.broadcasted_iota(jnp.int32, mask_sc.shape, 1)
        mask_sc[...] = jnp.where((c % hd) == (r // (2 * t)), 0.0, NEG_BIG)

    def online(s, m_sc, l_sc, acc_sc, pv):
        m_prev = m_sc[...]
        m_new = jnp.maximum(m_prev, jnp.max(s, axis=1, keepdims=True))
        alpha = jnp.exp(m_prev - m_new)
        pr = jnp.exp(s - m_new)
        l_sc[...] = alpha * l_sc[...] + jnp.sum(pr, axis=1, keepdims=True)
        acc_sc[...] = alpha * acc_sc[...] + pv(pr.astype(BF16))
        m_sc[...] = m_new

    def diff_page(k_rows, v_rows, causal):
        nr = k_rows.shape[0]
        kb = k_rows.astype(BF16)
        vb = v_rows.astype(BF16)
        s = lax.dot_general(qd_sc[...], kb, (((1,), (1,)), ((), ())), preferred_element_type=F32)
        r = lax.broadcasted_iota(jnp.int32, (rows_d, nr), 0)
        c = lax.broadcasted_iota(jnp.int32, (rows_d, nr), 1)
        ok = (c % hd) == (r // (2 * t))
        if causal:
            ok = ok & ((c // hd) <= (r % t))
        s = jnp.where(ok, s, NEG_BIG)
        online(s, md_sc, ld_sc, accd_sc, lambda pb: jnp.dot(pb, vb, preferred_element_type=F32))

    def fox_page(kt, vt, cum_in, causal):
        nt = kt.shape[1]
        kb = kt.astype(BF16)
        vb = vt.astype(BF16)
        s = jnp.dot(qf_sc[...], kb, preferred_element_type=F32)
        cum = cum_in + off_sc[...]
        bias = jnp.concatenate([jnp.broadcast_to(cum[h:h + 1, :], (t, nt)) for h in range(hf)], axis=0)
        s = s - bias
        if causal:
            r = lax.broadcasted_iota(jnp.int32, (rows_f, nt), 0)
            c = lax.broadcasted_iota(jnp.int32, (rows_f, nt), 1)
            s = jnp.where(c <= (r % t), s, NEG_BIG)
        online(s, mf_sc, lf_sc, accf_sc,
               lambda pb: lax.dot_general(pb, vb, (((1,), (1,)), ((), ())), preferred_element_type=F32))
        off_sc[...] = off_sc[...] + cum_in[:, nt - 1:nt]

    @pl.when(j < last)
    def _():
        for cp in page_copies(b, j, slot):
            cp.wait()
        rp = kd_refs[0].shape[0]
        pg = kf_refs[0].shape[1]
        for i in range(npg):
            kd_sc[i * rp:(i + 1) * rp, :] = kd_refs[i][...].astype(BF16)
            vd_sc[i * rp:(i + 1) * rp, :] = vd_refs[i][...].astype(BF16)
            kf_sc[:, i * pg:(i + 1) * pg] = kf_refs[i][...].astype(BF16)
            vf_sc[:, i * pg:(i + 1) * pg] = vf_refs[i][...].astype(BF16)
        s = lax.dot_general(qd_sc[...], kd_sc[...], (((1,), (1,)), ((), ())), preferred_element_type=F32)
        s = s + jnp.concatenate([mask_sc[...]] * npg, axis=1)
        online(s, md_sc, ld_sc, accd_sc, lambda pb: jnp.dot(pb, vd_sc[...], preferred_element_type=F32))
        sf = jnp.dot(qf_sc[...], kf_sc[...], preferred_element_type=F32)
        off = off_sc[...]
        biases = []
        for i in range(npg):
            cum_in = lf_refs[i][...]
            cum = cum_in + off
            biases.append(jnp.concatenate([jnp.broadcast_to(cum[h:h + 1, :], (t, pg)) for h in range(hf)], axis=0))
            off = off + cum_in[:, pg - 1:pg]
        off_sc[...] = off
        sf = sf - jnp.concatenate(biases, axis=1)
        online(sf, mf_sc, lf_sc, accf_sc,
               lambda pb: lax.dot_general(pb, vf_sc[...], (((1,), (1,)), ((), ())), preferred_element_type=F32))

    @pl.when(j == last)
    def _():
        diff_page(kdn_ref[...], vdn_ref[...], True)
        fox_page(kfn_ref[...], vfn_ref[...], lfn_ref[...], True)
        od = accd_sc[...] / ld_sc[...]
        for h in range(hd):
            o1 = od[(2 * h) * t:(2 * h + 1) * t, :]
            o2 = od[(2 * h + 1) * t:(2 * h + 2) * t, :]
            o = _diff_combine(o1, o2, lamv_ref[...], gsub_ref[...], lam_init)
            od_ref[:, h * LANES:(h + 1) * LANES] = o.astype(od_ref.dtype)
        of_all = accf_sc[...] / lf_sc[...]
        lane_f = lax.broadcasted_iota(jnp.int32, (t, wf), 1)
        acc = jnp.zeros((t, wf), F32)
        for h in range(hf):
            acc = acc + jnp.where(lane_f // HEAD_DIM == h, of_all[h * t:(h + 1) * t, :], 0.0)
        of_ref[...] = acc.astype(of_ref.dtype)


def _decode(page_table, ckd, cvd, ckf, cvf, clf, qd, qf, kdn, vdn, kfn, vfn, lfn, lamv, gsub, lam_init):
    b, n_pages = page_table.shape
    t, wd = qd.shape[1], qd.shape[2]
    wf = qf.shape[2]
    hd = wd // LANES
    hf = wf // HEAD_DIM
    npg = math.gcd(PAGES_PER_STEP, n_pages)
    steps = n_pages // npg

    per_b = lambda arr: pl.BlockSpec((None,) + arr.shape[1:], lambda bb, jj, pt: (bb, 0, 0))
    full = lambda arr: pl.BlockSpec(arr.shape, lambda bb, jj, pt: (0, 0))
    in_specs = [pl.BlockSpec(memory_space=pl.ANY)] * 5
    args = [ckd, cvd, ckf, cvf, clf]
    for arr in (qd, qf, kdn, vdn, kfn, vfn, lfn):
        in_specs.append(per_b(arr))
        args.append(arr)
    in_specs += [full(lamv), full(gsub)]
    args += [lamv, gsub]
    rows_d = hd * 2 * t
    rows_f = hf * t
    return pl.pallas_call(
        functools.partial(_decode_body, n_pages_step=npg, n_diff_heads=hd, n_fox_heads=hf, n_new=t,
                          lam_init=lam_init),
        grid_spec=pltpu.PrefetchScalarGridSpec(
            num_scalar_prefetch=1,
            grid=(b, steps + 1),
            in_specs=in_specs,
            out_specs=[pl.BlockSpec((None, t, wd), lambda bb, jj, pt: (bb, 0, 0)),
                       pl.BlockSpec((None, t, wf), lambda bb, jj, pt: (bb, 0, 0))],
            scratch_shapes=[pltpu.VMEM((rows_d, LANES), BF16), pltpu.VMEM((rows_f, wf), BF16),
                            pltpu.VMEM((rows_d, 1), F32), pltpu.VMEM((rows_d, 1), F32),
                            pltpu.VMEM((rows_d, LANES), F32),
                            pltpu.VMEM((rows_f, 1), F32), pltpu.VMEM((rows_f, 1), F32),
                            pltpu.VMEM((rows_f, wf), F32),
                            pltpu.VMEM((hf, 1), F32),
                            pltpu.VMEM((npg * ckd.shape[1], LANES), BF16),
                            pltpu.VMEM((npg * ckd.shape[1], LANES), BF16),
                            pltpu.VMEM((wf, npg * ckf.shape[2]), BF16),
                            pltpu.VMEM((wf, npg * ckf.shape[2]), BF16),
                            pltpu.VMEM((rows_d, ckd.shape[1]), F32),
                            pltpu.VMEM((2, npg) + ckd.shape[1:], F32), pltpu.VMEM((2, npg) + cvd.shape[1:], F32),
                            pltpu.VMEM((2, npg) + ckf.shape[1:], F32), pltpu.VMEM((2, npg) + cvf.shape[1:], F32),
                            pltpu.VMEM((2, npg) + clf.shape[1:], F32),
                            pltpu.SemaphoreType.DMA((2,))]),
        out_shape=[jax.ShapeDtypeStruct((b, t, wd), BF16), jax.ShapeDtypeStruct((b, t, wf), BF16)],
        compiler_params=_cparams(("arbitrary", "arbitrary")),
        name="decode",
    )(page_table, *args)


def _outproj_body(od_ref, of_ref, x_ref, ga_ref, sc_ref, sh_ref, g_ref, wo_ref, wr_ref, br_ref, *rest, n_experts):
    x1_ref, h_ref, idx_ref, gate_ref, cnt_ref = rest[-5:]
    tm = x_ref.shape[0]
    wd = od_ref.shape[1]
    o = (jnp.dot(od_ref[...], wo_ref[:wd, :], preferred_element_type=F32)
         + jnp.dot(of_ref[...], wo_ref[wd:, :], preferred_element_type=F32))
    x1 = x_ref[...] + ga_ref[...] * o
    x1_ref[...] = x1
    h = _rms_mod(x1, g_ref[...], sc_ref[...], sh_ref[...])
    _store_row_tiles(h_ref, h)
    logits = jnp.dot(h, wr_ref[...], preferred_element_type=F32, precision=lax.Precision.HIGHEST) + br_ref[...]
    lane = lax.broadcasted_iota(jnp.int32, (tm, LANES), 1).astype(F32)
    cur = jnp.where(lane < n_experts, logits, -jnp.inf)
    idx_out = jnp.zeros((tm, LANES), F32)
    val_out = jnp.zeros((tm, LANES), F32)
    top = None
    den = jnp.zeros((tm, 1), F32)
    cnt = jnp.zeros((1, LANES), F32)
    for k in range(TOP_K):
        mx = jnp.max(cur, axis=1, keepdims=True)
        sel = jnp.min(jnp.where(cur == mx, lane, float(LANES)), axis=1, keepdims=True)
        if k == 0:
            top = mx
        e = jnp.exp(mx - top)
        den = den + e
        idx_out = jnp.where(lane == k, sel, idx_out)
        val_out = jnp.where(lane == k, e, val_out)
        picked = lane == sel
        cnt = cnt + jnp.sum(picked.astype(F32), axis=0, keepdims=True)
        cur = jnp.where(picked, -jnp.inf, cur)
    idx_ref[...] = idx_out.astype(jnp.int32)
    gate_ref[...] = val_out / den
    cnt_ref[...] = jnp.broadcast_to(cnt, cnt_ref.shape)


def _outproj(od, of, x, ga, sc, sh, g2, w_out, w_r, b_r, rows_per_group, n_experts, h_rows, h_row_offset, h_buf):
    n, d = x.shape
    tm = ROW_TILE
    rt = d // LANES
    off_tiles = h_row_offset // tm
    tiles_per_group = rows_per_group // tm if rows_per_group >= tm else None
    if tiles_per_group is not None:
        mod_spec = pl.BlockSpec((None, 1, d), lambda i: (i // tiles_per_group, 0, 0))
    else:
        mod_spec = pl.BlockSpec((None, tm, d), lambda i: (i, 0, 0))
    row = lambda w: pl.BlockSpec((tm, w), lambda i: (i, 0))
    full = lambda a: pl.BlockSpec(a.shape, lambda i: (0,) * a.ndim)
    in_specs = [row(od.shape[1]), row(of.shape[1]), row(d), mod_spec, mod_spec, mod_spec,
                full(g2), full(w_out), full(w_r), full(b_r)]
    args = [od, of, x, ga, sc, sh, g2, w_out, w_r, b_r]
    aliases = {}
    if h_buf is not None:
        in_specs.append(pl.BlockSpec(memory_space=pl.ANY))
        args.append(h_buf)
        aliases = {len(args) - 1: 1}
    return pl.pallas_call(
        functools.partial(_outproj_body, n_experts=n_experts),
        grid=(n // tm,),
        in_specs=in_specs,
        out_specs=[row(d), pl.BlockSpec((tm * rt, LANES), lambda i: (i + off_tiles, 0)), row(LANES), row(LANES),
                   pl.BlockSpec((None, 8, LANES), lambda i: (i, 0, 0))],
        out_shape=[jax.ShapeDtypeStruct((n, d), F32), jax.ShapeDtypeStruct((h_rows * rt, LANES), F32),
                   jax.ShapeDtypeStruct((n, LANES), jnp.int32), jax.ShapeDtypeStruct((n, LANES), F32),
                   jax.ShapeDtypeStruct((n // tm, 8, LANES), F32)],
        input_output_aliases=aliases,
        compiler_params=_cparams(("parallel",)),
        name="outproj",
    )(*args)


def _moe_body(be_ref, tok_next_ref, tok_first_ref, dst_prev_ref, dst_last_ref, gate_ref, h_hbm,
              wgu_ref, bgu_ref, wd_ref, bd_ref, y_hbm, tok_sm, dst_sm, xbuf, ybuf, sem_idx, sem_g, sem_s):
    del be_ref
    i = pl.program_id(0)
    n = pl.num_programs(0)
    rows = gate_ref.shape[0]
    d = wgu_ref.shape[0]
    rt = d // LANES
    slot = i % 2
    nslot = 1 - slot

    def load_indices(src_ref, dst_smem):
        cp = pltpu.make_async_copy(src_ref.at[0], dst_smem, sem_idx)
        cp.start()
        cp.wait()

    def gather_rows(buf_slot):
        for r in range(rows):
            src = pl.multiple_of(tok_sm[0, r] * rt, rt)
            pltpu.make_async_copy(h_hbm.at[pl.ds(src, rt), :], xbuf.at[buf_slot, pl.ds(r * rt, rt), :],
                                  sem_g.at[buf_slot]).start(priority=r % DMA_QUEUES)

    def scatter_row(buf_slot, r, queue=0):
        dst = pl.multiple_of(dst_sm[0, r] * rt, rt)
        pltpu.make_async_copy(ybuf.at[buf_slot, pl.ds(r * rt, rt), :], y_hbm.at[pl.ds(dst, rt), :],
                              sem_s.at[buf_slot]).start(priority=queue)

    def scatter_rows(buf_slot):
        for r in range(rows):
            scatter_row(buf_slot, r, (r + 1) % DMA_QUEUES)

    def wait_gather(buf_slot):
        pltpu.make_async_copy(h_hbm.at[pl.ds(0, rows * rt), :], xbuf.at[buf_slot], sem_g.at[buf_slot]).wait()

    def wait_scatter(buf_slot):
        pltpu.make_async_copy(ybuf.at[buf_slot], y_hbm.at[pl.ds(0, rows * rt), :], sem_s.at[buf_slot]).wait()

    @pl.when(i == 0)
    def _():
        load_indices(tok_first_ref, tok_sm)
        gather_rows(0)
        ybuf[1] = jnp.zeros(ybuf.shape[1:], F32)

    @pl.when(i >= 1)
    def _():
        wait_scatter(slot)

    wait_gather(slot)
    load_indices(tok_next_ref, tok_sm)
    load_indices(dst_prev_ref, dst_sm)
    gather_rows(nslot)
    scatter_rows(nslot)

    x = _load_row_tiles(xbuf, rows, d, lead=(slot,)).astype(BF16)
    gu = jnp.dot(x, wgu_ref[...], preferred_element_type=F32) + bgu_ref[...]
    dff = gu.shape[1] // 2
    g = jnp.minimum(gu[:, :dff], SWIGLU_LIMIT)
    u = jnp.clip(gu[:, dff:], -SWIGLU_LIMIT, SWIGLU_LIMIT)
    act = (u + 1.0) * g * jax.nn.sigmoid(SWIGLU_ALPHA * g)
    y = jnp.dot(act.astype(BF16), wd_ref[...], preferred_element_type=F32) + bd_ref[...]
    _store_row_tiles(ybuf, y * gate_ref[...], lead=(slot,))

    @pl.when(i == n - 1)
    def _():
        load_indices(dst_last_ref, dst_sm)

        def scatter(r, carry):
            scatter_row(slot, r)
            return carry
        lax.fori_loop(0, rows, scatter, 0, unroll=8)
        wait_gather(nslot)
        wait_scatter(nslot)
        wait_scatter(slot)


def _moe(h, idx, gates, counts, w_gu, b_gu, w_down, b_down):
    d = w_gu.shape[1]
    rt = d // LANES
    n = h.shape[0] // rt
    n_experts = w_gu.shape[0]
    blk = EXPERT_ROWS
    n_asg = n * TOP_K
    n_blocks = -(-n_asg // blk) + n_experts
    n_slots = n_blocks * blk
    n_pad = n_slots - n_asg
    order = jnp.argsort(idx.reshape(-1)).astype(jnp.int32)
    start = jnp.cumsum(counts) - counts
    padded = (counts + blk - 1) // blk * blk
    pad_end = jnp.cumsum(padded)
    pad_start = pad_end - padded
    block_first = jnp.arange(n_blocks, dtype=jnp.int32) * blk
    block_expert = jnp.minimum(jnp.sum(block_first[:, None] >= pad_end[None, :], axis=1),
                               n_experts - 1).astype(jnp.int32)
    b_pos = (block_first - pad_start[block_expert])[:, None]
    b_cnt = counts[block_expert][:, None]
    b_start = start[block_expert][:, None]
    b_pad0 = (pad_start - start)[block_expert][:, None]
    pos = b_pos + jnp.arange(blk, dtype=jnp.int32)[None, :]
    valid = pos < b_cnt
    src = order[jnp.clip(b_start + pos, 0, n_asg - 1)]
    pad_row = n_asg + b_pad0 + (pos - b_cnt)
    slot_tok = jnp.where(valid, src // TOP_K, 0).astype(jnp.int32)
    slot_dst = jnp.where(valid, (src % TOP_K) * n + src // TOP_K, pad_row).astype(jnp.int32)
    slot_gate = jnp.where(valid, gates.reshape(-1)[src], 0.0)
    dump_block = n_asg + n_pad + jnp.arange(blk, dtype=jnp.int32)
    slot_tok = slot_tok.reshape(n_blocks, 1, blk)
    slot_dst = jnp.concatenate([dump_block[None, :], slot_dst], axis=0).reshape(n_blocks + 1, 1, blk)

    y = pl.pallas_call(
        _moe_body,
        grid_spec=pltpu.PrefetchScalarGridSpec(
            num_scalar_prefetch=1,
            grid=(n_blocks,),
            in_specs=[
                pl.BlockSpec((1, 1, blk), lambda i, be: (jnp.minimum(i + 1, n_blocks - 1), 0, 0)),
                pl.BlockSpec((1, 1, blk), lambda i, be: (0, 0, 0)),
                pl.BlockSpec((1, 1, blk), lambda i, be: (i, 0, 0)),
                pl.BlockSpec((1, 1, blk), lambda i, be: (n_blocks, 0, 0)),
                pl.BlockSpec((blk, 1), lambda i, be: (i, 0)),
                pl.BlockSpec(memory_space=pl.ANY),
                pl.BlockSpec((None, d, w_gu.shape[2]), lambda i, be: (be[i], 0, 0)),
                pl.BlockSpec((None, 1, b_gu.shape[2]), lambda i, be: (be[i], 0, 0)),
                pl.BlockSpec((None, w_down.shape[1], d), lambda i, be: (be[i], 0, 0)),
                pl.BlockSpec((None, 1, d), lambda i, be: (be[i], 0, 0)),
            ],
            out_specs=pl.BlockSpec(memory_space=pl.ANY),
            scratch_shapes=[pltpu.SMEM((1, blk), jnp.int32), pltpu.SMEM((1, blk), jnp.int32),
                            pltpu.VMEM((2, blk * rt, LANES), F32), pltpu.VMEM((2, blk * rt, LANES), F32),
                            pltpu.SemaphoreType.DMA, pltpu.SemaphoreType.DMA((2,)),
                            pltpu.SemaphoreType.DMA((2,))]),
        out_shape=jax.ShapeDtypeStruct(((n_asg + n_pad + blk) * rt, LANES), F32),
        compiler_params=_cparams(("arbitrary",)),
        name="moe",
    )(block_expert, slot_tok, slot_tok, slot_dst, slot_dst, slot_gate.reshape(n_slots, 1),
      h, w_gu, b_gu, w_down, b_down)
    return y


def _final_body(x_ref, ga_ref, y0_ref, y1_ref, y2_ref, y3_ref, g_ref, o_ref):
    tm, d = x_ref.shape
    y0, y1, y2, y3 = (_load_row_tiles(r, tm, d) for r in (y0_ref, y1_ref, y2_ref, y3_ref))
    moe = (y0 + y1) + (y2 + y3)
    x = x_ref[...] + ga_ref[...] * moe
    ms = jnp.mean(x * x, axis=-1, keepdims=True)
    o_ref[...] = x * lax.rsqrt(ms + NORM_EPS) * g_ref[...]


def _final(x1, ga, y_un, g_final, rows_per_group, n_total, row_offset):
    n, d = x1.shape
    tm = ROW_TILE
    nt = n // tm
    nt_total = n_total // tm
    off_tiles = row_offset // tm
    tiles_per_group = rows_per_group // tm if rows_per_group >= tm else None
    if tiles_per_group is not None:
        mod_spec = pl.BlockSpec((None, 1, d), lambda i: (i // tiles_per_group, 0, 0))
    else:
        mod_spec = pl.BlockSpec((None, tm, d), lambda i: (i, 0, 0))
    yk = lambda k: pl.BlockSpec((tm * (d // LANES), LANES), lambda i: (k * nt_total + off_tiles + i, 0))
    return pl.pallas_call(
        _final_body,
        grid=(nt,),
        in_specs=[pl.BlockSpec((tm, d), lambda i: (i, 0)), mod_spec,
                  yk(0), yk(1), yk(2), yk(3), pl.BlockSpec((1, d), lambda i: (0, 0))],
        out_specs=pl.BlockSpec((tm, d), lambda i: (i, 0)),
        out_shape=jax.ShapeDtypeStruct((n, d), F32),
        compiler_params=_cparams(("parallel",)),
        name="final",
    )(x1, ga, y_un, y_un, y_un, y_un, g_final)


def _rope_tables(pos):
    half = HEAD_DIM // 2
    inv = 1.0 / (ROPE_THETA ** (jnp.arange(half, dtype=F32) / half))
    ang = pos.astype(F32)[:, None] * inv[None, :]
    cos = jnp.tile(jnp.cos(ang), (1, LANES // half))
    sin = jnp.sin(ang)
    sin = jnp.tile(jnp.concatenate([-sin, sin], axis=1), (1, LANES // HEAD_DIM))
    return cos, sin


def _mods_for(m, rows_per_group):
    g = m.shape[0]
    parts = jnp.split(m, 6, axis=-1)
    if rows_per_group >= ROW_TILE:
        return [p[:, None, :] for p in parts]
    groups_per_tile = ROW_TILE // rows_per_group
    return [jnp.repeat(p, rows_per_group, axis=0).reshape(g // groups_per_tile, ROW_TILE, -1) for p in parts]


def kernel(x_prompt, x_sample, c_prompt, c_sample, cache_k_diff, cache_v_diff, cache_k_fox, cache_v_fox, cache_logf_fox, page_table, w_ada, b_ada, g_norm1, g_norm2, w_in, b_fgate, lam_q1, lam_k1, lam_q2, lam_k2, g_subln, w_out, w_router, b_router, w_gu, b_gu, w_down, b_down, g_final):
    bp, s, d = x_prompt.shape
    bs, t, _ = x_sample.shape
    depth = w_ada.shape[0]
    assert depth == 1
    l = 0
    lam_init = 0.8 - 0.6 * math.exp(-0.3 * l)
    n_pool, page = cache_k_diff.shape[1], cache_k_diff.shape[2]
    hd = cache_k_diff.shape[3]
    hf = cache_k_fox.shape[3]
    wd = hd * 2 * HEAD_DIM
    wf = hf * HEAD_DIM
    n_experts = w_router.shape[2]
    past = page_table.shape[1] * page

    scale = HEAD_DIM ** -0.5
    w_in_l = w_in[l]
    col_scale = jnp.concatenate([jnp.full((wd,), scale, F32), jnp.ones((2 * wd,), F32),
                                 jnp.full((wf,), scale, F32), jnp.ones((2 * wf,), F32)])
    w_main = (w_in_l[:, :3 * wd + 3 * wf] * col_scale[None, :]).astype(BF16)
    w_f = jnp.pad(w_in_l[:, 3 * wd + 3 * wf:], ((0, 0), (0, LANES - hf))).astype(BF16)
    b_f = jnp.pad(b_fgate[l], (0, LANES - hf)).reshape(1, LANES)
    w_out_b = w_out[l].astype(BF16)
    w_r = jnp.pad(w_router[l], ((0, 0), (0, LANES - n_experts)))
    b_r = jnp.pad(b_router[l], (0, LANES - n_experts)).reshape(1, LANES)
    w_gu_b = w_gu[l].astype(BF16)
    w_down_b = w_down[l].astype(BF16)
    b_gu_l = b_gu[l][:, None, :]
    b_down_l = b_down[l][:, None, :]
    g1 = g_norm1[l].reshape(1, d)
    g2 = g_norm2[l].reshape(1, d)
    gf = g_final.reshape(1, d)
    lamv = jnp.stack([lam_q1[l], lam_k1[l], lam_q2[l], lam_k2[l]], axis=0)
    gsub = g_subln[l].reshape(1, 2 * HEAD_DIM)

    mods = _adaln(jnp.concatenate([c_prompt, c_sample], axis=0), w_ada[l], b_ada[l])
    sh1p, sc1p, ga1p, sh2p, sc2p, ga2p = _mods_for(mods[:bp], s)
    sh1s, sc1s, ga1s, sh2s, sc2s, ga2s = _mods_for(mods[bp:], t)

    cos_p, sin_p = _rope_tables(jnp.arange(s))
    xp = x_prompt.reshape(bp * s, d)
    (qd, kd32, kd16, vd32, vd16, qf, kf32, kf16, vf32, vf16, lfp) = _inproj(
        xp, sc1p, sh1p, g1, cos_p, sin_p, w_main, w_f, b_f, s, s // ROW_TILE)
    logf_p = lfp[:, :hf]
    chunks = s // LANES
    lf_rows = jnp.transpose(logf_p.reshape(bp, s, hf), (0, 2, 1)).reshape(bp * hf * chunks, LANES)
    cum = _cumsum_rows(lf_rows, hf * chunks, chunks).reshape(bp, hf // 2, 2, s)
    b3 = lambda a: a.reshape(bp, s, -1)
    od_p = _flash(b3(qd), b3(kd16), b3(vd16), (lamv, gsub), fox=False, lam_init=lam_init)
    of_p = _flash(b3(qf), b3(kf16), b3(vf16), (cum,), fox=True, lam_init=lam_init)
    n_p, n_s = bp * s, bs * t
    n_tot = n_p + n_s
    x1p, h2, idxp, gatep, cntp = _outproj(od_p.reshape(n_p, wd), of_p.reshape(n_p, wf), xp,
                                          ga1p, sc2p, sh2p, g2, w_out_b, w_r, b_r, s, n_experts, n_tot, 0,
                                          jnp.zeros((n_tot * (d // LANES), LANES), F32))

    cos_s, sin_s = _rope_tables(past + jnp.arange(t))
    reps = ROW_TILE // t
    cos_s = jnp.tile(cos_s, (reps, 1))
    sin_s = jnp.tile(sin_s, (reps, 1))
    xs = x_sample.reshape(bs * t, d)
    (qds, kd32s, _u0, vd32s, _u1, qfs, kf32s, _u2, vf32s, _u3, lfs) = _inproj(
        xs, sc1s, sh1s, g1, cos_s, sin_s, w_main, w_f, b_f, t, 1)
    logf_s = lfs[:, :hf]
    ckd = cache_k_diff[l].reshape(n_pool, page * hd, 2 * HEAD_DIM)
    cvd = cache_v_diff[l].reshape(n_pool, page * hd, 2 * HEAD_DIM)
    ckf = jnp.transpose(cache_k_fox[l], (0, 2, 3, 1)).reshape(n_pool, wf, page)
    cvf = jnp.transpose(cache_v_fox[l], (0, 2, 3, 1)).reshape(n_pool, wf, page)
    clf = jnp.transpose(cache_logf_fox[l], (0, 2, 1)).reshape(n_pool * hf, page)
    clf_cum = _cumsum_rows(clf, math.gcd(2048, n_pool * hf), 1).reshape(n_pool, hf, page)
    tp = 16
    padt = lambda a: jnp.pad(a.reshape(bs, t, -1), ((0, 0), (0, tp - t), (0, 0)))
    kdn = padt(kd32s).reshape(bs, tp * hd, 2 * HEAD_DIM)
    vdn = padt(vd32s).reshape(bs, tp * hd, 2 * HEAD_DIM)
    padl = lambda a: jnp.pad(jnp.transpose(a.reshape(bs, t, -1), (0, 2, 1)), ((0, 0), (0, 0), (0, LANES - t)))
    kfn = padl(kf32s)
    vfn = padl(vf32s)
    lfn = _cumsum_rows(padl(logf_s).reshape(bs * hf, LANES), bs * hf, 1).reshape(bs, hf, LANES)
    od_s, of_s = _decode(page_table, ckd, cvd, ckf, cvf, clf_cum, qds.reshape(bs, t, wd), qfs.reshape(bs, t, wf),
                         kdn, vdn, kfn, vfn, lfn, lamv, gsub, lam_init)
    x1s, h2, idxs, gates, cnts = _outproj(od_s.reshape(n_s, wd), of_s.reshape(n_s, wf), xs,
                                          ga1s, sc2s, sh2s, g2, w_out_b, w_r, b_r, t, n_experts, n_tot, n_p, h2)

    counts = (jnp.sum(cntp[:, 0, :n_experts], axis=0) + jnp.sum(cnts[:, 0, :n_experts], axis=0)).astype(jnp.int32)
    y_un = _moe(h2, jnp.concatenate([idxp[:, :TOP_K], idxs[:, :TOP_K]], axis=0),
                jnp.concatenate([gatep[:, :TOP_K], gates[:, :TOP_K]], axis=0), counts,
                w_gu_b, b_gu_l, w_down_b, b_down_l)
    y_prompt = _final(x1p, ga2p, y_un, gf, s, n_tot, 0).reshape(bp, s, d)
    y_sample = _final(x1s, ga2s, y_un, gf, t, n_tot, n_p).reshape(bs, t, d)

    rp = lambda a, h_, c_: a.reshape(1, bp, s, h_, c_)
    rs = lambda a, h_, c_: a.reshape(1, bs, t, h_, c_)
    rpt = lambda a: jnp.transpose(a.reshape(bp, hf, HEAD_DIM, s), (0, 3, 1, 2))[None]
    return (y_prompt, y_sample,
            rp(kd32, hd, 2 * HEAD_DIM), rp(vd32, hd, 2 * HEAD_DIM), rpt(kf32), rpt(vf32),
            logf_p.reshape(1, bp, s, hf),
            rs(kd32s, hd, 2 * HEAD_DIM), rs(vd32s, hd, 2 * HEAD_DIM), rs(kf32s, hf, HEAD_DIM), rs(vf32s, hf, HEAD_DIM),
            logf_s.reshape(1, bs, t, hf))
```

```python
import functools
import math

import jax
import jax.numpy as jnp
import numpy as np
from jax import lax
from jax.experimental import pallas as pl
from jax.experimental.pallas import tpu as pltpu

HEAD_DIM = 64
LANES = 128
ROPE_THETA = 10000.0
NORM_EPS = 1e-5
TOP_K = 4
SWIGLU_ALPHA = 1.702
SWIGLU_LIMIT = 7.0
NEG_BIG = -1e30
ROW_TILE = 512
ATT_TILE = 1024
ATT_SUB_Q = 256
ATT_SUB_K = 512
DMA_QUEUES = 2
EXPERT_ROWS = 256
PAGES_PER_STEP = 8
VMEM_LIMIT = 56 * 1024 * 1024

F32 = jnp.float32
BF16 = jnp.bfloat16


def _cparams(sem):
    return pltpu.CompilerParams(dimension_semantics=sem, vmem_limit_bytes=VMEM_LIMIT)


def _store_row_tiles(ref, val, lead=()):
    rows, d = val.shape
    k = d // LANES
    for j in range(k):
        ref[lead + (pl.ds(j, rows, stride=k), slice(None))] = val[:, j * LANES:(j + 1) * LANES]


def _load_row_tiles(ref, rows, d, lead=()):
    k = d // LANES
    return jnp.concatenate([ref[lead + (pl.ds(j, rows, stride=k), slice(None))] for j in range(k)], axis=1)


def _adaln_body(c_ref, w_ref, b_ref, o_ref):
    c = c_ref[...]
    s = (c * jax.nn.sigmoid(c)).astype(BF16)
    o_ref[...] = jnp.dot(s, w_ref[...].astype(BF16), preferred_element_type=F32) + b_ref[...]


def _adaln(c, w, b):
    n, d = c.shape
    nout = w.shape[1]
    tn = 1024
    return pl.pallas_call(
        _adaln_body,
        grid=(nout // tn,),
        in_specs=[pl.BlockSpec((n, d), lambda j: (0, 0)),
                  pl.BlockSpec((d, tn), lambda j: (0, j)),
                  pl.BlockSpec((1, tn), lambda j: (0, j))],
        out_specs=pl.BlockSpec((n, tn), lambda j: (0, j)),
        out_shape=jax.ShapeDtypeStruct((n, nout), F32),
        compiler_params=_cparams(("arbitrary",)),
        name="adaln",
    )(c, w, b.reshape(1, nout))


def _rms_mod(x, g, sc, sh):
    ms = jnp.mean(x * x, axis=-1, keepdims=True)
    return (x * lax.rsqrt(ms + NORM_EPS) * g) * (1.0 + sc) + sh


def _inproj_body(x_ref, sc_ref, sh_ref, g_ref, cos_ref, sin_ref, w_ref, wf_ref, bf_ref,
                 qd_ref, kd32_ref, kd16_ref, vd32_ref, vd16_ref,
                 qf_ref, kf32_ref, kf16_ref, vf32_ref, vf16_ref, lf_ref, *, fox_transposed):
    tm = x_ref.shape[0]
    hb = _rms_mod(x_ref[...], g_ref[...], sc_ref[...], sh_ref[...]).astype(BF16)
    cos = cos_ref[...]
    sin = sin_ref[...]
    lane = lax.broadcasted_iota(jnp.int32, (tm, LANES), 1)
    first_half = (lane % HEAD_DIM) < (HEAD_DIM // 2)

    def rope(blk):
        partner = jnp.where(first_half, pltpu.roll(blk, LANES - HEAD_DIM // 2, 1),
                            pltpu.roll(blk, HEAD_DIM // 2, 1))
        return blk * cos + partner * sin

    width = qd_ref.shape[1]
    nblk = width // LANES
    outs = ((qd_ref,), (kd32_ref, kd16_ref), (vd32_ref, vd16_ref),
            (qf_ref,), (kf32_ref, kf16_ref), (vf32_ref, vf16_ref))
    for gidx, refs in enumerate(outs):
        z = jnp.dot(hb, w_ref[:, gidx * width:(gidx + 1) * width], preferred_element_type=F32)
        if fox_transposed and gidx >= 4:
            refs[0][...] = z.T
            refs = refs[1:]
        for j in range(nblk):
            blk = z[:, j * LANES:(j + 1) * LANES]
            if gidx < 2:
                blk = rope(blk)
            for r in refs:
                if r is kd32_ref or r is vd32_ref:
                    r[pl.ds(j, tm, stride=nblk), :] = blk
                else:
                    r[:, j * LANES:(j + 1) * LANES] = blk.astype(r.dtype)
    fl = jnp.dot(hb, wf_ref[...], preferred_element_type=F32) + bf_ref[...]
    lf_ref[...] = jax.nn.log_sigmoid(fl)


def _inproj(x, sc, sh, g1, cos, sin, w_main, w_f, b_f, rows_per_group, pos_tiles):
    n, d = x.shape
    tm = ROW_TILE
    width = w_main.shape[1] // 6
    tiles_per_group = rows_per_group // tm if rows_per_group >= tm else None
    if tiles_per_group is not None:
        mod_spec = pl.BlockSpec((None, 1, d), lambda i: (i // tiles_per_group, 0, 0))
    else:
        mod_spec = pl.BlockSpec((None, tm, d), lambda i: (i, 0, 0))
    tab_spec = pl.BlockSpec((tm, LANES), lambda i: (i % pos_tiles, 0))
    row = lambda w: pl.BlockSpec((tm, w), lambda i: (i, 0))
    full = lambda a: pl.BlockSpec(a.shape, lambda i: (0,) * a.ndim)
    o16 = jax.ShapeDtypeStruct((n, width), BF16)
    nblk = width // LANES
    ort = jax.ShapeDtypeStruct((n * nblk, LANES), F32)
    rt_spec = pl.BlockSpec((tm * nblk, LANES), lambda i: (i, 0))
    fox_transposed = tiles_per_group is not None
    if fox_transposed:
        ofx = jax.ShapeDtypeStruct((n // rows_per_group, width, rows_per_group), F32)
        fx_spec = pl.BlockSpec((None, width, tm), lambda i: (i // tiles_per_group, 0, i % tiles_per_group))
    else:
        ofx = jax.ShapeDtypeStruct((n, width), F32)
        fx_spec = row(width)
    return pl.pallas_call(
        functools.partial(_inproj_body, fox_transposed=fox_transposed),
        grid=(n // tm,),
        in_specs=[row(d), mod_spec, mod_spec, full(g1), tab_spec, tab_spec,
                  full(w_main), full(w_f), full(b_f)],
        out_specs=[row(width), rt_spec, row(width), rt_spec, row(width),
                   row(width), fx_spec, row(width), fx_spec, row(width), row(LANES)],
        out_shape=[o16, ort, o16, ort, o16, o16, ofx, o16, ofx, o16,
                   jax.ShapeDtypeStruct((n, LANES), F32)],
        compiler_params=_cparams(("parallel",)),
        name="inproj",
    )(x, sc, sh, g1, cos, sin, w_main, w_f, b_f)


def _cumsum_body(x_ref, o_ref, *, chunks_per_row):
    r = x_ref.shape[0]
    i0 = lax.broadcasted_iota(jnp.int32, (LANES, LANES), 0)
    i1 = lax.broadcasted_iota(jnp.int32, (LANES, LANES), 1)
    upper = (i0 <= i1).astype(F32)
    cum = jnp.dot(x_ref[...], upper, preferred_element_type=F32, precision=lax.Precision.HIGHEST)
    if chunks_per_row > 1:
        r0 = lax.broadcasted_iota(jnp.int32, (r, r), 0)
        r1 = lax.broadcasted_iota(jnp.int32, (r, r), 1)
        prev = ((r0 // chunks_per_row == r1 // chunks_per_row) & (r1 < r0)).astype(F32)
        tot = jnp.broadcast_to(cum[:, LANES - 1:LANES], (r, LANES))
        cum = cum + jnp.dot(prev, tot, preferred_element_type=F32, precision=lax.Precision.HIGHEST)
    o_ref[...] = cum


def _cumsum_rows(x, rows_per_block, chunks_per_row):
    n = x.shape[0]
    return pl.pallas_call(
        functools.partial(_cumsum_body, chunks_per_row=chunks_per_row),
        grid=(n // rows_per_block,),
        in_specs=[pl.BlockSpec((rows_per_block, LANES), lambda i: (i, 0))],
        out_specs=pl.BlockSpec((rows_per_block, LANES), lambda i: (i, 0)),
        out_shape=jax.ShapeDtypeStruct((n, LANES), F32),
        compiler_params=_cparams(("parallel",)),
        name="cumsum",
    )(x)


def _lambda_from(lamv, lam_init):
    a = jnp.sum(lamv[0:1, :] * lamv[1:2, :], axis=1, keepdims=True)
    b = jnp.sum(lamv[2:3, :] * lamv[3:4, :], axis=1, keepdims=True)
    return jnp.exp(a) - jnp.exp(b) + lam_init


def _diff_combine(o1, o2, lamv, gsub, lam_init):
    o = o1 - _lambda_from(lamv, lam_init) * o2
    ms = jnp.mean(o * o, axis=-1, keepdims=True)
    return (o * lax.rsqrt(ms + NORM_EPS) * gsub) * (1.0 - lam_init)


def _flash_body(qi_tab, ki_tab, *refs, fox, lam_init):
    if fox:
        q_ref, k_ref, v_ref, cum_ref, o_ref, m_sc, l_sc, acc_sc = refs
    else:
        q_ref, k_ref, v_ref, lamv_ref, gsub_ref, o_ref, m_sc, l_sc, acc_sc = refs
    p = pl.program_id(2)
    qi = qi_tab[p]
    ki = ki_tab[p]
    tq = q_ref.shape[0]
    tk = k_ref.shape[0]

    @pl.when(ki == 0)
    def _():
        m_sc[...] = jnp.full(m_sc.shape, -jnp.inf, F32)
        l_sc[...] = jnp.zeros(l_sc.shape, F32)
        acc_sc[...] = jnp.zeros(acc_sc.shape, F32)

    lane = lax.broadcasted_iota(jnp.int32, (tq, LANES), 1)
    low = lane < HEAD_DIM

    sq = min(ATT_SUB_Q, tq)
    sk = min(ATT_SUB_K, tk)

    def step(masked):
        nq = tq // sq
        lo = lax.broadcasted_iota(jnp.int32, (sq, LANES), 1) < HEAD_DIM
        state = {}
        qcs = {}
        for qb in range(nq):
            qs = slice(qb * sq, (qb + 1) * sq)
            q = q_ref[qs, :]
            zero = jnp.zeros_like(q)
            for c in range(2):
                qcs[qb, c] = jnp.where(lo, q, zero) if c == 0 else jnp.where(lo, zero, q)
                state[qb, c] = (m_sc[c, qs, :], l_sc[c, qs, :], acc_sc[c, qs, :])
        pieces = [(kb, qb, c) for kb in range(tk // sk) for qb in range(nq) for c in range(2)
                  if not (masked and kb * sk > qb * sq + sq - 1)]

        def scores(kb, qb, c):
            ks = slice(kb * sk, (kb + 1) * sk)
            s = lax.dot_general(qcs[qb, c], k_ref[ks, :], (((1,), (1,)), ((), ())), preferred_element_type=F32)
            if fox:
                s = s - cum_ref[c:c + 1, ks]
            if masked and kb * sk + sk - 1 > qb * sq:
                row = lax.broadcasted_iota(jnp.int32, (sq, sk), 0) + qb * sq
                col = lax.broadcasted_iota(jnp.int32, (sq, sk), 1) + kb * sk
                s = jnp.where(col <= row, s, NEG_BIG)
            return s

        s_next = scores(*pieces[0])
        for idx, (kb, qb, c) in enumerate(pieces):
            s = s_next
            if idx + 1 < len(pieces):
                s_next = scores(*pieces[idx + 1])
            m_prev, l_prev, acc_prev = state[qb, c]
            m_new = jnp.maximum(m_prev, jnp.broadcast_to(jnp.max(s, axis=1, keepdims=True), (sq, LANES)))
            alpha = jnp.exp(m_prev - m_new)
            pr = jnp.exp(s - jnp.concatenate([m_new] * (sk // LANES), axis=1))
            v_ext = jnp.concatenate([v_ref[kb * sk:(kb + 1) * sk, :], jnp.ones((sk, LANES), BF16)], axis=1)
            r = jnp.dot(pr.astype(BF16), v_ext, preferred_element_type=F32)
            l_new = alpha * l_prev + r[:, LANES:]
            acc_new = alpha * acc_prev + r[:, :LANES]
            state[qb, c] = (m_new, l_new, acc_new)
        for qb in range(nq):
            qs = slice(qb * sq, (qb + 1) * sq)
            for c in range(2):
                m_sc[c, qs, :], l_sc[c, qs, :], acc_sc[c, qs, :] = state[qb, c]

    @pl.when(ki < qi)
    def _():
        step(False)

    @pl.when(ki == qi)
    def _():
        step(True)
        o1 = acc_sc[0] / l_sc[0]
        o2 = acc_sc[1] / l_sc[1]
        if fox:
            o = jnp.where(low, o1, o2)
        else:
            o = _diff_combine(o1, o2, lamv_ref[...], gsub_ref[...], lam_init)
        o_ref[...] = o.astype(o_ref.dtype)


def _flash(q, k, v, extra, *, fox, lam_init):
    b, s, w = q.shape
    g = w // LANES
    t = min(ATT_TILE, s)
    nt = s // t
    pairs = [(i, j) for i in range(nt) for j in range(i + 1)]
    qi_tab = jnp.asarray([pq for pq, _ in pairs], jnp.int32)
    ki_tab = jnp.asarray([pk for _, pk in pairs], jnp.int32)
    q_spec = pl.BlockSpec((None, t, LANES), lambda bb, gg, p, qt, kt: (bb, qt[p], gg))
    kv_spec = pl.BlockSpec((None, t, LANES), lambda bb, gg, p, qt, kt: (bb, kt[p], gg))
    if fox:
        extra_specs = [pl.BlockSpec((None, None, 2, t), lambda bb, gg, p, qt, kt: (bb, gg, 0, kt[p]))]
    else:
        extra_specs = [pl.BlockSpec(extra[0].shape, lambda bb, gg, p, qt, kt: (0, 0)),
                       pl.BlockSpec(extra[1].shape, lambda bb, gg, p, qt, kt: (0, 0))]
    return pl.pallas_call(
        functools.partial(_flash_body, fox=fox, lam_init=lam_init),
        grid_spec=pltpu.PrefetchScalarGridSpec(
            num_scalar_prefetch=2,
            grid=(b, g, len(pairs)),
            in_specs=[q_spec, kv_spec, kv_spec] + extra_specs,
            out_specs=pl.BlockSpec((None, t, LANES), lambda bb, gg, p, qt, kt: (bb, qt[p], gg)),
            scratch_shapes=[pltpu.VMEM((2, t, LANES), F32), pltpu.VMEM((2, t, LANES), F32),
                            pltpu.VMEM((2, t, LANES), F32)]),
        out_shape=jax.ShapeDtypeStruct((b, s, w), BF16),
        compiler_params=_cparams(("parallel", "parallel", "arbitrary")),
        name="flash_fox" if fox else "flash_diff",
    )(qi_tab, ki_tab, q, k, v, *extra)


def _decode_body(pt_ref, *refs, n_pages_step, n_diff_heads, n_fox_heads, n_new, lam_init):
    npg = n_pages_step
    (ckd_hbm, cvd_hbm, ckf_hbm, cvf_hbm, clf_hbm,
     qd_ref, qf_ref, kdn_ref, vdn_ref, kfn_ref, vfn_ref, lfn_ref, lamv_ref, gsub_ref,
     od_ref, of_ref, qd_sc, qf_sc, md_sc, ld_sc, accd_sc, mf_sc, lf_sc, accf_sc, off_sc,
     kd_sc, vd_sc, kf_sc, vf_sc, mask_sc, kd_raw, vd_raw, kf_raw, vf_raw, lf_raw, sem_pg) = refs
    b = pl.program_id(0)
    nb = pl.num_programs(0)
    j = pl.program_id(1)
    last = pl.num_programs(1) - 1
    slot = (b * last + j) % 2

    def page_copies(bq, jq, sl):
        out = []
        for i in range(npg):
            page = pt_ref[bq, jq * npg + i]
            for hbm, raw in ((ckd_hbm, kd_raw), (cvd_hbm, vd_raw), (ckf_hbm, kf_raw), (cvf_hbm, vf_raw),
                             (clf_hbm, lf_raw)):
                out.append(pltpu.make_async_copy(hbm.at[page], raw.at[sl, i], sem_pg.at[sl]))
        return out

    def fetch(bq, jq, sl):
        for cp in page_copies(bq, jq, sl):
            cp.start()

    @pl.when((b == 0) & (j == 0))
    def _():
        fetch(0, 0, 0)

    @pl.when(j + 1 < last)
    def _():
        fetch(b, j + 1, 1 - slot)

    @pl.when((j + 1 == last) & (b + 1 < nb))
    def _():
        fetch(b + 1, 0, 1 - slot)

    kd_refs = [kd_raw.at[slot, i] for i in range(npg)]
    vd_refs = [vd_raw.at[slot, i] for i in range(npg)]
    kf_refs = [kf_raw.at[slot, i] for i in range(npg)]
    vf_refs = [vf_raw.at[slot, i] for i in range(npg)]
    lf_refs = [lf_raw.at[slot, i] for i in range(npg)]
    hd, hf, t = n_diff_heads, n_fox_heads, n_new
    rows_d = hd * 2 * t
    rows_f = hf * t
    wf = hf * HEAD_DIM

    @pl.when(j == 0)
    def _():
        lane = lax.broadcasted_iota(jnp.int32, (t, LANES), 1)
        qd = qd_ref[...]
        for h in range(hd):
            blk = qd[:, h * LANES:(h + 1) * LANES]
            zero = jnp.zeros_like(blk)
            qd_sc[(2 * h) * t:(2 * h + 1) * t, :] = jnp.where(lane < HEAD_DIM, blk, zero)
            qd_sc[(2 * h + 1) * t:(2 * h + 2) * t, :] = jnp.where(lane < HEAD_DIM, zero, blk)
        qf = qf_ref[...]
        lane_f = lax.broadcasted_iota(jnp.int32, (t, wf), 1)
        for h in range(hf):
            qf_sc[h * t:(h + 1) * t, :] = jnp.where(lane_f // HEAD_DIM == h, qf, jnp.zeros_like(qf))
        md_sc[...] = jnp.full(md_sc.shape, -jnp.inf, F32)
        ld_sc[...] = jnp.zeros(ld_sc.shape, F32)
        accd_sc[...] = jnp.zeros(accd_sc.shape, F32)
        mf_sc[...] = jnp.full(mf_sc.shape, -jnp.inf, F32)
        lf_sc[...] = jnp.zeros(lf_sc.shape, F32)
        accf_sc[...] = jnp.zeros(accf_sc.shape, F32)
        off_sc[...] = jnp.zeros(off_sc.shape, F32)
        r = lax.broadcasted_iota(jnp.int32, mask_sc.shape, 0)
        c = lax.broadcasted_iota(jnp.int32, mask_sc.shape, 1)
        mask_sc[...] = jnp.where((c % hd) == (r // (2 * t)), 0.0, NEG_BIG)

    def online(s, m_sc, l_sc, acc_sc, pv):
        m_prev = m_sc[...]
        m_new = jnp.maximum(m_prev, jnp.max(s, axis=1, keepdims=True))
        alpha = jnp.exp(m_prev - m_new)
        pr = jnp.exp(s - m_new)
        l_sc[...] = alpha * l_sc[...] + jnp.sum(pr, axis=1, keepdims=True)
        acc_sc[...] = alpha * acc_sc[...] + pv(pr.astype(BF16))
        m_sc[...] = m_new

    def diff_page(k_rows, v_rows, causal):
        nr = k_rows.shape[0]
        kb = k_rows.astype(BF16)
        vb = v_rows.astype(BF16)
        s = lax.dot_general(qd_sc[...], kb, (((1,), (1,)), ((), ())), preferred_element_type=F32)
        r = lax.broadcasted_iota(jnp.int32, (rows_d, nr), 0)
        c = lax.broadcasted_iota(jnp.int32, (rows_d, nr), 1)
        ok = (c % hd) == (r // (2 * t))
        if causal:
            ok = ok & ((c // hd) <= (r % t))
        s = jnp.where(ok, s, NEG_BIG)
        online(s, md_sc, ld_sc, accd_sc, lambda pb: jnp.dot(pb, vb, preferred_element_type=F32))

    def fox_page(kt, vt, cum_in, causal):
        nt = kt.shape[1]
        kb = kt.astype(BF16)
        vb = vt.astype(BF16)
        s = jnp.dot(qf_sc[...], kb, preferred_element_type=F32)
        cum = cum_in + off_sc[...]
        bias = jnp.concatenate([jnp.broadcast_to(cum[h:h + 1, :], (t, nt)) for h in range(hf)], axis=0)
        s = s - bias
        if causal:
            r = lax.broadcasted_iota(jnp.int32, (rows_f, nt), 0)
            c = lax.broadcasted_iota(jnp.int32, (rows_f, nt), 1)
            s = jnp.where(c <= (r % t), s, NEG_BIG)
        online(s, mf_sc, lf_sc, accf_sc,
               lambda pb: lax.dot_general(pb, vb, (((1,), (1,)), ((), ())), preferred_element_type=F32))
        off_sc[...] = off_sc[...] + cum_in[:, nt - 1:nt]

    @pl.when(j < last)
    def _():
        for cp in page_copies(b, j, slot):
            cp.wait()
        rp = kd_refs[0].shape[0]
        pg = kf_refs[0].shape[1]
        for i in range(npg):
            kd_sc[i * rp:(i + 1) * rp, :] = kd_refs[i][...].astype(BF16)
            vd_sc[i * rp:(i + 1) * rp, :] = vd_refs[i][...].astype(BF16)
            kf_sc[:, i * pg:(i + 1) * pg] = kf_refs[i][...].astype(BF16)
            vf_sc[:, i * pg:(i + 1) * pg] = vf_refs[i][...].astype(BF16)
        s = lax.dot_general(qd_sc[...], kd_sc[...], (((1,), (1,)), ((), ())), preferred_element_type=F32)
        s = s + jnp.concatenate([mask_sc[...]] * npg, axis=1)
        online(s, md_sc, ld_sc, accd_sc, lambda pb: jnp.dot(pb, vd_sc[...], preferred_element_type=F32))
        sf = jnp.dot(qf_sc[...], kf_sc[...], preferred_element_type=F32)
        off = off_sc[...]
        biases = []
        for i in range(npg):
            cum_in = lf_refs[i][...]
            cum = cum_in + off
            biases.append(jnp.concatenate([jnp.broadcast_to(cum[h:h + 1, :], (t, pg)) for h in range(hf)], axis=0))
            off = off + cum_in[:, pg - 1:pg]
        off_sc[...] = off
        sf = sf - jnp.concatenate(biases, axis=1)
        online(sf, mf_sc, lf_sc, accf_sc,
               lambda pb: lax.dot_general(pb, vf_sc[...], (((1,), (1,)), ((), ())), preferred_element_type=F32))

    @pl.when(j == last)
    def _():
        diff_page(kdn_ref[...], vdn_ref[...], True)
        fox_page(kfn_ref[...], vfn_ref[...], lfn_ref[...], True)
        od = accd_sc[...] / ld_sc[...]
        for h in range(hd):
            o1 = od[(2 * h) * t:(2 * h + 1) * t, :]
            o2 = od[(2 * h + 1) * t:(2 * h + 2) * t, :]
            o = _diff_combine(o1, o2, lamv_ref[...], gsub_ref[...], lam_init)
            od_ref[:, h * LANES:(h + 1) * LANES] = o.astype(od_ref.dtype)
        of_all = accf_sc[...] / lf_sc[...]
        lane_f = lax.broadcasted_iota(jnp.int32, (t, wf), 1)
        acc = jnp.zeros((t, wf), F32)
        for h in range(hf):
            acc = acc + jnp.where(lane_f // HEAD_DIM == h, of_all[h * t:(h + 1) * t, :], 0.0)
        of_ref[...] = acc.astype(of_ref.dtype)


def _decode(page_table, ckd, cvd, ckf, cvf, clf, qd, qf, kdn, vdn, kfn, vfn, lfn, lamv, gsub, lam_init):
    b, n_pages = page_table.shape
    t, wd = qd.shape[1], qd.shape[2]
    wf = qf.shape[2]
    hd = wd // LANES
    hf = wf // HEAD_DIM
    npg = math.gcd(PAGES_PER_STEP, n_pages)
    steps = n_pages // npg

    per_b = lambda arr: pl.BlockSpec((None,) + arr.shape[1:], lambda bb, jj, pt: (bb, 0, 0))
    full = lambda arr: pl.BlockSpec(arr.shape, lambda bb, jj, pt: (0, 0))
    in_specs = [pl.BlockSpec(memory_space=pl.ANY)] * 5
    args = [ckd, cvd, ckf, cvf, clf]
    for arr in (qd, qf, kdn, vdn, kfn, vfn, lfn):
        in_specs.append(per_b(arr))
        args.append(arr)
    in_specs += [full(lamv), full(gsub)]
    args += [lamv, gsub]
    rows_d = hd * 2 * t
    rows_f = hf * t
    return pl.pallas_call(
        functools.partial(_decode_body, n_pages_step=npg, n_diff_heads=hd, n_fox_heads=hf, n_new=t,
                          lam_init=lam_init),
        grid_spec=pltpu.PrefetchScalarGridSpec(
            num_scalar_prefetch=1,
            grid=(b, steps + 1),
            in_specs=in_specs,
            out_specs=[pl.BlockSpec((None, t, wd), lambda bb, jj, pt: (bb, 0, 0)),
                       pl.BlockSpec((None, t, wf), lambda bb, jj, pt: (bb, 0, 0))],
            scratch_shapes=[pltpu.VMEM((rows_d, LANES), BF16), pltpu.VMEM((rows_f, wf), BF16),
                            pltpu.VMEM((rows_d, 1), F32), pltpu.VMEM((rows_d, 1), F32),
                            pltpu.VMEM((rows_d, LANES), F32),
                            pltpu.VMEM((rows_f, 1), F32), pltpu.VMEM((rows_f, 1), F32),
                            pltpu.VMEM((rows_f, wf), F32),
                            pltpu.VMEM((hf, 1), F32),
                            pltpu.VMEM((npg * ckd.shape[1], LANES), BF16),
                            pltpu.VMEM((npg * ckd.shape[1], LANES), BF16),
                            pltpu.VMEM((wf, npg * ckf.shape[2]), BF16),
                            pltpu.VMEM((wf, npg * ckf.shape[2]), BF16),
                            pltpu.VMEM((rows_d, ckd.shape[1]), F32),
                            pltpu.VMEM((2, npg) + ckd.shape[1:], F32), pltpu.VMEM((2, npg) + cvd.shape[1:], F32),
                            pltpu.VMEM((2, npg) + ckf.shape[1:], F32), pltpu.VMEM((2, npg) + cvf.shape[1:], F32),
                            pltpu.VMEM((2, npg) + clf.shape[1:], F32),
                            pltpu.SemaphoreType.DMA((2,))]),
        out_shape=[jax.ShapeDtypeStruct((b, t, wd), BF16), jax.ShapeDtypeStruct((b, t, wf), BF16)],
        compiler_params=_cparams(("arbitrary", "arbitrary")),
        name="decode",
    )(page_table, *args)


def _outproj_body(od_ref, of_ref, x_ref, ga_ref, sc_ref, sh_ref, g_ref, wo_ref, wr_ref, br_ref, *rest, n_experts):
    x1_ref, h_ref, idx_ref, gate_ref, cnt_ref = rest[-5:]
    tm = x_ref.shape[0]
    wd = od_ref.shape[1]
    o = (jnp.dot(od_ref[...], wo_ref[:wd, :], preferred_element_type=F32)
         + jnp.dot(of_ref[...], wo_ref[wd:, :], preferred_element_type=F32))
    x1 = x_ref[...] + ga_ref[...] * o
    x1_ref[...] = x1
    h = _rms_mod(x1, g_ref[...], sc_ref[...], sh_ref[...])
    _store_row_tiles(h_ref, h)
    logits = jnp.dot(h, wr_ref[...], preferred_element_type=F32, precision=lax.Precision.HIGHEST) + br_ref[...]
    lane = lax.broadcasted_iota(jnp.int32, (tm, LANES), 1).astype(F32)
    cur = jnp.where(lane < n_experts, logits, -jnp.inf)
    idx_out = jnp.zeros((tm, LANES), F32)
    val_out = jnp.zeros((tm, LANES), F32)
    top = None
    den = jnp.zeros((tm, 1), F32)
    cnt = jnp.zeros((1, LANES), F32)
    for k in range(TOP_K):
        mx = jnp.max(cur, axis=1, keepdims=True)
        sel = jnp.min(jnp.where(cur == mx, lane, float(LANES)), axis=1, keepdims=True)
        if k == 0:
            top = mx
        e = jnp.exp(mx - top)
        den = den + e
        idx_out = jnp.where(lane == k, sel, idx_out)
        val_out = jnp.where(lane == k, e, val_out)
        picked = lane == sel
        cnt = cnt + jnp.sum(picked.astype(F32), axis=0, keepdims=True)
        cur = jnp.where(picked, -jnp.inf, cur)
    idx_ref[...] = idx_out.astype(jnp.int32)
    gate_ref[...] = val_out / den
    cnt_ref[...] = jnp.broadcast_to(cnt, cnt_ref.shape)


def _outproj(od, of, x, ga, sc, sh, g2, w_out, w_r, b_r, rows_per_group, n_experts, h_rows, h_row_offset, h_buf):
    n, d = x.shape
    tm = ROW_TILE
    rt = d // LANES
    off_tiles = h_row_offset // tm
    tiles_per_group = rows_per_group // tm if rows_per_group >= tm else None
    if tiles_per_group is not None:
        mod_spec = pl.BlockSpec((None, 1, d), lambda i: (i // tiles_per_group, 0, 0))
    else:
        mod_spec = pl.BlockSpec((None, tm, d), lambda i: (i, 0, 0))
    row = lambda w: pl.BlockSpec((tm, w), lambda i: (i, 0))
    full = lambda a: pl.BlockSpec(a.shape, lambda i: (0,) * a.ndim)
    in_specs = [row(od.shape[1]), row(of.shape[1]), row(d), mod_spec, mod_spec, mod_spec,
                full(g2), full(w_out), full(w_r), full(b_r)]
    args = [od, of, x, ga, sc, sh, g2, w_out, w_r, b_r]
    aliases = {}
    if h_buf is not None:
        in_specs.append(pl.BlockSpec(memory_space=pl.ANY))
        args.append(h_buf)
        aliases = {len(args) - 1: 1}
    return pl.pallas_call(
        functools.partial(_outproj_body, n_experts=n_experts),
        grid=(n // tm,),
        in_specs=in_specs,
        out_specs=[row(d), pl.BlockSpec((tm * rt, LANES), lambda i: (i + off_tiles, 0)), row(LANES), row(LANES),
                   pl.BlockSpec((None, 8, LANES), lambda i: (i, 0, 0))],
        out_shape=[jax.ShapeDtypeStruct((n, d), F32), jax.ShapeDtypeStruct((h_rows * rt, LANES), F32),
                   jax.ShapeDtypeStruct((n, LANES), jnp.int32), jax.ShapeDtypeStruct((n, LANES), F32),
                   jax.ShapeDtypeStruct((n // tm, 8, LANES), F32)],
        input_output_aliases=aliases,
        compiler_params=_cparams(("parallel",)),
        name="outproj",
    )(*args)


def _moe_body(be_ref, meta_next_ref, meta_first_ref, tok_first_ref, dst_last_ref, gate_ref, h_hbm,
              wgu_ref, bgu_ref, wd_ref, bd_ref, y_hbm, idx_sm, tmp_sm, xbuf, ybuf, sem_idx, sem_g, sem_s):
    del be_ref
    i = pl.program_id(0)
    n = pl.num_programs(0)
    rows = gate_ref.shape[0]
    d = wgu_ref.shape[0]
    rt = d // LANES
    slot = i % 2
    nslot = 1 - slot

    def meta_copy(src_ref, parity):
        return pltpu.make_async_copy(src_ref.at[0], idx_sm.at[parity], sem_idx)

    def tmp_load(src_ref):
        cp = pltpu.make_async_copy(src_ref.at[0], tmp_sm, sem_idx)
        cp.start()
        cp.wait()

    def gather_rows(buf_slot, tok_at):
        for r in range(rows):
            src = pl.multiple_of(tok_at(r) * rt, rt)
            pltpu.make_async_copy(h_hbm.at[pl.ds(src, rt), :], xbuf.at[buf_slot, pl.ds(r * rt, rt), :],
                                  sem_g.at[buf_slot]).start(priority=r % DMA_QUEUES)

    def scatter_row(buf_slot, r, dst_row, queue=0):
        dst = pl.multiple_of(dst_row * rt, rt)
        pltpu.make_async_copy(ybuf.at[buf_slot, pl.ds(r * rt, rt), :], y_hbm.at[pl.ds(dst, rt), :],
                              sem_s.at[buf_slot]).start(priority=queue)

    def wait_gather(buf_slot):
        pltpu.make_async_copy(h_hbm.at[pl.ds(0, rows * rt), :], xbuf.at[buf_slot], sem_g.at[buf_slot]).wait()

    def wait_scatter(buf_slot):
        pltpu.make_async_copy(ybuf.at[buf_slot], y_hbm.at[pl.ds(0, rows * rt), :], sem_s.at[buf_slot]).wait()

    @pl.when(i == 0)
    def _():
        tmp_load(tok_first_ref)
        gather_rows(0, lambda r: tmp_sm[0, r])
        first = meta_copy(meta_first_ref, 0)
        first.start()
        first.wait()
        ybuf[1] = jnp.zeros(ybuf.shape[1:], F32)

    @pl.when(i >= 1)
    def _():
        wait_scatter(slot)

    @pl.when(i + 1 < n)
    def _():
        meta_copy(meta_next_ref, nslot).start()

    wait_gather(slot)
    gather_rows(nslot, lambda r: idx_sm[slot, 0, r])
    for r in range(rows):
        scatter_row(nslot, r, idx_sm[slot, 1, r], (r + 1) % DMA_QUEUES)

    x = _load_row_tiles(xbuf, rows, d, lead=(slot,)).astype(BF16)
    gu = jnp.dot(x, wgu_ref[...], preferred_element_type=F32) + bgu_ref[...]
    dff = gu.shape[1] // 2
    g = jnp.minimum(gu[:, :dff], SWIGLU_LIMIT)
    u = jnp.clip(gu[:, dff:], -SWIGLU_LIMIT, SWIGLU_LIMIT)
    act = (u + 1.0) * g * jax.nn.sigmoid(SWIGLU_ALPHA * g)
    y = jnp.dot(act.astype(BF16), wd_ref[...], preferred_element_type=F32) + bd_ref[...]
    _store_row_tiles(ybuf, y * gate_ref[...], lead=(slot,))

    @pl.when(i + 1 < n)
    def _():
        meta_copy(meta_next_ref, nslot).wait()

    @pl.when(i == n - 1)
    def _():
        tmp_load(dst_last_ref)

        def scatter(r, carry):
            scatter_row(slot, r, tmp_sm[0, r])
            return carry
        lax.fori_loop(0, rows, scatter, 0, unroll=8)
        wait_gather(nslot)
        wait_scatter(nslot)
        wait_scatter(slot)


def _moe(h, idx, gates, counts, w_gu, b_gu, w_down, b_down):
    d = w_gu.shape[1]
    rt = d // LANES
    n = h.shape[0] // rt
    n_experts = w_gu.shape[0]
    blk = EXPERT_ROWS
    n_asg = n * TOP_K
    n_blocks = -(-n_asg // blk) + n_experts
    n_slots = n_blocks * blk
    n_pad = n_slots - n_asg
    order = jnp.argsort(idx.reshape(-1)).astype(jnp.int32)
    start = jnp.cumsum(counts) - counts
    padded = (counts + blk - 1) // blk * blk
    pad_end = jnp.cumsum(padded)
    pad_start = pad_end - padded
    block_first = jnp.arange(n_blocks, dtype=jnp.int32) * blk
    block_expert = jnp.minimum(jnp.sum(block_first[:, None] >= pad_end[None, :], axis=1),
                               n_experts - 1).astype(jnp.int32)
    b_pos = (block_first - pad_start[block_expert])[:, None]
    b_cnt = counts[block_expert][:, None]
    b_start = start[block_expert][:, None]
    b_pad0 = (pad_start - start)[block_expert][:, None]
    pos = b_pos + jnp.arange(blk, dtype=jnp.int32)[None, :]
    valid = pos < b_cnt
    src = order[jnp.clip(b_start + pos, 0, n_asg - 1)]
    pad_row = n_asg + b_pad0 + (pos - b_cnt)
    slot_tok = jnp.where(valid, src // TOP_K, 0).astype(jnp.int32)
    slot_dst = jnp.where(valid, (src % TOP_K) * n + src // TOP_K, pad_row).astype(jnp.int32)
    slot_gate = jnp.where(valid, gates.reshape(-1)[src], 0.0)
    dump_block = n_asg + n_pad + jnp.arange(blk, dtype=jnp.int32)
    tok_next = jnp.concatenate([slot_tok[1:], slot_tok[-1:]], axis=0)
    dst_prev = jnp.concatenate([dump_block[None, :], slot_dst[:-1]], axis=0)
    meta = jnp.stack([tok_next, dst_prev], axis=1)
    tok_first = slot_tok[:1].reshape(1, 1, blk)
    dst_last = slot_dst[-1:].reshape(1, 1, blk)

    y = pl.pallas_call(
        _moe_body,
        grid_spec=pltpu.PrefetchScalarGridSpec(
            num_scalar_prefetch=1,
            grid=(n_blocks,),
            in_specs=[
                pl.BlockSpec((1, 2, blk), lambda i, be: (jnp.minimum(i + 1, n_blocks - 1), 0, 0)),
                pl.BlockSpec((1, 2, blk), lambda i, be: (0, 0, 0)),
                pl.BlockSpec((1, 1, blk), lambda i, be: (0, 0, 0)),
                pl.BlockSpec((1, 1, blk), lambda i, be: (0, 0, 0)),
                pl.BlockSpec((blk, 1), lambda i, be: (i, 0)),
                pl.BlockSpec(memory_space=pl.ANY),
                pl.BlockSpec((None, d, w_gu.shape[2]), lambda i, be: (be[i], 0, 0)),
                pl.BlockSpec((None, 1, b_gu.shape[2]), lambda i, be: (be[i], 0, 0)),
                pl.BlockSpec((None, w_down.shape[1], d), lambda i, be: (be[i], 0, 0)),
                pl.BlockSpec((None, 1, d), lambda i, be: (be[i], 0, 0)),
            ],
            out_specs=pl.BlockSpec(memory_space=pl.ANY),
            scratch_shapes=[pltpu.SMEM((2, 2, blk), jnp.int32), pltpu.SMEM((1, blk), jnp.int32),
                            pltpu.VMEM((2, blk * rt, LANES), F32), pltpu.VMEM((2, blk * rt, LANES), F32),
                            pltpu.SemaphoreType.DMA, pltpu.SemaphoreType.DMA((2,)),
                            pltpu.SemaphoreType.DMA((2,))]),
        out_shape=jax.ShapeDtypeStruct(((n_asg + n_pad + blk) * rt, LANES), F32),
        compiler_params=_cparams(("arbitrary",)),
        name="moe",
    )(block_expert, meta, meta, tok_first, dst_last, slot_gate.reshape(n_slots, 1),
      h, w_gu, b_gu, w_down, b_down)
    return y


def _final_body(x_ref, ga_ref, y0_ref, y1_ref, y2_ref, y3_ref, g_ref, o_ref):
    tm, d = x_ref.shape
    y0, y1, y2, y3 = (_load_row_tiles(r, tm, d) for r in (y0_ref, y1_ref, y2_ref, y3_ref))
    moe = (y0 + y1) + (y2 + y3)
    x = x_ref[...] + ga_ref[...] * moe
    ms = jnp.mean(x * x, axis=-1, keepdims=True)
    o_ref[...] = x * lax.rsqrt(ms + NORM_EPS) * g_ref[...]


def _final(x1, ga, y_un, g_final, rows_per_group, n_total, row_offset):
    n, d = x1.shape
    tm = ROW_TILE
    nt = n // tm
    nt_total = n_total // tm
    off_tiles = row_offset // tm
    tiles_per_group = rows_per_group // tm if rows_per_group >= tm else None
    if tiles_per_group is not None:
        mod_spec = pl.BlockSpec((None, 1, d), lambda i: (i // tiles_per_group, 0, 0))
    else:
        mod_spec = pl.BlockSpec((None, tm, d), lambda i: (i, 0, 0))
    yk = lambda k: pl.BlockSpec((tm * (d // LANES), LANES), lambda i: (k * nt_total + off_tiles + i, 0))
    return pl.pallas_call(
        _final_body,
        grid=(nt,),
        in_specs=[pl.BlockSpec((tm, d), lambda i: (i, 0)), mod_spec,
                  yk(0), yk(1), yk(2), yk(3), pl.BlockSpec((1, d), lambda i: (0, 0))],
        out_specs=pl.BlockSpec((tm, d), lambda i: (i, 0)),
        out_shape=jax.ShapeDtypeStruct((n, d), F32),
        compiler_params=_cparams(("parallel",)),
        name="final",
    )(x1, ga, y_un, y_un, y_un, y_un, g_final)


def _rope_tables(pos):
    half = HEAD_DIM // 2
    inv = 1.0 / (ROPE_THETA ** (jnp.arange(half, dtype=F32) / half))
    ang = pos.astype(F32)[:, None] * inv[None, :]
    cos = jnp.tile(jnp.cos(ang), (1, LANES // half))
    sin = jnp.sin(ang)
    sin = jnp.tile(jnp.concatenate([-sin, sin], axis=1), (1, LANES // HEAD_DIM))
    return cos, sin


def _mods_for(m, rows_per_group):
    g = m.shape[0]
    parts = jnp.split(m, 6, axis=-1)
    if rows_per_group >= ROW_TILE:
        return [p[:, None, :] for p in parts]
    groups_per_tile = ROW_TILE // rows_per_group
    return [jnp.repeat(p, rows_per_group, axis=0).reshape(g // groups_per_tile, ROW_TILE, -1) for p in parts]


def kernel(x_prompt, x_sample, c_prompt, c_sample, cache_k_diff, cache_v_diff, cache_k_fox, cache_v_fox, cache_logf_fox, page_table, w_ada, b_ada, g_norm1, g_norm2, w_in, b_fgate, lam_q1, lam_k1, lam_q2, lam_k2, g_subln, w_out, w_router, b_router, w_gu, b_gu, w_down, b_down, g_final):
    bp, s, d = x_prompt.shape
    bs, t, _ = x_sample.shape
    depth = w_ada.shape[0]
    assert depth == 1
    l = 0
    lam_init = 0.8 - 0.6 * math.exp(-0.3 * l)
    n_pool, page = cache_k_diff.shape[1], cache_k_diff.shape[2]
    hd = cache_k_diff.shape[3]
    hf = cache_k_fox.shape[3]
    wd = hd * 2 * HEAD_DIM
    wf = hf * HEAD_DIM
    n_experts = w_router.shape[2]
    past = page_table.shape[1] * page

    scale = HEAD_DIM ** -0.5
    w_in_l = w_in[l]
    col_scale = jnp.concatenate([jnp.full((wd,), scale, F32), jnp.ones((2 * wd,), F32),
                                 jnp.full((wf,), scale, F32), jnp.ones((2 * wf,), F32)])
    w_main = (w_in_l[:, :3 * wd + 3 * wf] * col_scale[None, :]).astype(BF16)
    w_f = jnp.pad(w_in_l[:, 3 * wd + 3 * wf:], ((0, 0), (0, LANES - hf))).astype(BF16)
    b_f = jnp.pad(b_fgate[l], (0, LANES - hf)).reshape(1, LANES)
    w_out_b = w_out[l].astype(BF16)
    w_r = jnp.pad(w_router[l], ((0, 0), (0, LANES - n_experts)))
    b_r = jnp.pad(b_router[l], (0, LANES - n_experts)).reshape(1, LANES)
    w_gu_b = w_gu[l].astype(BF16)
    w_down_b = w_down[l].astype(BF16)
    b_gu_l = b_gu[l][:, None, :]
    b_down_l = b_down[l][:, None, :]
    g1 = g_norm1[l].reshape(1, d)
    g2 = g_norm2[l].reshape(1, d)
    gf = g_final.reshape(1, d)
    lamv = jnp.stack([lam_q1[l], lam_k1[l], lam_q2[l], lam_k2[l]], axis=0)
    gsub = g_subln[l].reshape(1, 2 * HEAD_DIM)

    mods = _adaln(jnp.concatenate([c_prompt, c_sample], axis=0), w_ada[l], b_ada[l])
    sh1p, sc1p, ga1p, sh2p, sc2p, ga2p = _mods_for(mods[:bp], s)
    sh1s, sc1s, ga1s, sh2s, sc2s, ga2s = _mods_for(mods[bp:], t)

    cos_p, sin_p = _rope_tables(jnp.arange(s))
    xp = x_prompt.reshape(bp * s, d)
    (qd, kd32, kd16, vd32, vd16, qf, kf32, kf16, vf32, vf16, lfp) = _inproj(
        xp, sc1p, sh1p, g1, cos_p, sin_p, w_main, w_f, b_f, s, s // ROW_TILE)
    logf_p = lfp[:, :hf]
    chunks = s // LANES
    lf_rows = jnp.transpose(logf_p.reshape(bp, s, hf), (0, 2, 1)).reshape(bp * hf * chunks, LANES)
    cum = _cumsum_rows(lf_rows, hf * chunks, chunks).reshape(bp, hf // 2, 2, s)
    b3 = lambda a: a.reshape(bp, s, -1)
    od_p = _flash(b3(qd), b3(kd16), b3(vd16), (lamv, gsub), fox=False, lam_init=lam_init)
    of_p = _flash(b3(qf), b3(kf16), b3(vf16), (cum,), fox=True, lam_init=lam_init)
    n_p, n_s = bp * s, bs * t
    n_tot = n_p + n_s
    x1p, h2, idxp, gatep, cntp = _outproj(od_p.reshape(n_p, wd), of_p.reshape(n_p, wf), xp,
                                          ga1p, sc2p, sh2p, g2, w_out_b, w_r, b_r, s, n_experts, n_tot, 0,
                                          jnp.zeros((n_tot * (d // LANES), LANES), F32))

    cos_s, sin_s = _rope_tables(past + jnp.arange(t))
    reps = ROW_TILE // t
    cos_s = jnp.tile(cos_s, (reps, 1))
    sin_s = jnp.tile(sin_s, (reps, 1))
    xs = x_sample.reshape(bs * t, d)
    (qds, kd32s, _u0, vd32s, _u1, qfs, kf32s, _u2, vf32s, _u3, lfs) = _inproj(
        xs, sc1s, sh1s, g1, cos_s, sin_s, w_main, w_f, b_f, t, 1)
    logf_s = lfs[:, :hf]
    ckd = cache_k_diff[l].reshape(n_pool, page * hd, 2 * HEAD_DIM)
    cvd = cache_v_diff[l].reshape(n_pool, page * hd, 2 * HEAD_DIM)
    ckf = jnp.transpose(cache_k_fox[l], (0, 2, 3, 1)).reshape(n_pool, wf, page)
    cvf = jnp.transpose(cache_v_fox[l], (0, 2, 3, 1)).reshape(n_pool, wf, page)
    clf = jnp.transpose(cache_logf_fox[l], (0, 2, 1)).reshape(n_pool * hf, page)
    clf_cum = _cumsum_rows(clf, math.gcd(2048, n_pool * hf), 1).reshape(n_pool, hf, page)
    tp = 16
    padt = lambda a: jnp.pad(a.reshape(bs, t, -1), ((0, 0), (0, tp - t), (0, 0)))
    kdn = padt(kd32s).reshape(bs, tp * hd, 2 * HEAD_DIM)
    vdn = padt(vd32s).reshape(bs, tp * hd, 2 * HEAD_DIM)
    padl = lambda a: jnp.pad(jnp.transpose(a.reshape(bs, t, -1), (0, 2, 1)), ((0, 0), (0, 0), (0, LANES - t)))
    kfn = padl(kf32s)
    vfn = padl(vf32s)
    lfn = _cumsum_rows(padl(logf_s).reshape(bs * hf, LANES), bs * hf, 1).reshape(bs, hf, LANES)
    od_s, of_s = _decode(page_table, ckd, cvd, ckf, cvf, clf_cum, qds.reshape(bs, t, wd), qfs.reshape(bs, t, wf),
                         kdn, vdn, kfn, vfn, lfn, lamv, gsub, lam_init)
    x1s, h2, idxs, gates, cnts = _outproj(od_s.reshape(n_s, wd), of_s.reshape(n_s, wf), xs,
                                          ga1s, sc2s, sh2s, g2, w_out_b, w_r, b_r, t, n_experts, n_tot, n_p, h2)

    counts = (jnp.sum(cntp[:, 0, :n_experts], axis=0) + jnp.sum(cnts[:, 0, :n_experts], axis=0)).astype(jnp.int32)
    y_un = _moe(h2, jnp.concatenate([idxp[:, :TOP_K], idxs[:, :TOP_K]], axis=0),
                jnp.concatenate([gatep[:, :TOP_K], gates[:, :TOP_K]], axis=0), counts,
                w_gu_b, b_gu_l, w_down_b, b_down_l)
    y_prompt = _final(x1p, ga2p, y_un, gf, s, n_tot, 0).reshape(bp, s, d)
    y_sample = _final(x1s, ga2s, y_un, gf, t, n_tot, n_p).reshape(bs, t, d)

    rp = lambda a, h_, c_: a.reshape(1, bp, s, h_, c_)
    rs = lambda a, h_, c_: a.reshape(1, bs, t, h_, c_)
    rpt = lambda a: jnp.transpose(a.reshape(bp, hf, HEAD_DIM, s), (0, 3, 1, 2))[None]
    return (y_prompt, y_sample,
            rp(kd32, hd, 2 * HEAD_DIM), rp(vd32, hd, 2 * HEAD_DIM), rpt(kf32), rpt(vf32),
            logf_p.reshape(1, bp, s, hf),
            rs(kd32s, hd, 2 * HEAD_DIM), rs(vd32s, hd, 2 * HEAD_DIM), rs(kf32s, hf, HEAD_DIM), rs(vf32s, hf, HEAD_DIM),
            logf_s.reshape(1, bs, t, hf))
```
